```python
import math
import jax
import jax.numpy as jnp
from jax import lax
import numpy as np

D_MODEL = 1024
BATCH = 8
SEQ = 4096
DEPTH = 4

HEAD_DIM = 64
DIL_GROUPS = ((128, 1), (512, 4), (2048, 16))
HEADS_PER_DIL_GROUP = 4
N_HEADS_A = HEADS_PER_DIL_GROUP * len(DIL_GROUPS)
N_HEADS_B = 6
N_HEADS_C = 6
N_HEADS = N_HEADS_A + N_HEADS_B + N_HEADS_C
MIX_WIDTH = N_HEADS * HEAD_DIM
WIDTH_A = HEADS_PER_DIL_GROUP * HEAD_DIM
WIDTH_B = N_HEADS_B * HEAD_DIM
WIDTH_C = N_HEADS_C * HEAD_DIM
ROPE_DIM = HEAD_DIM // 4
ROPE_THETA = 500000.0
MOBA_BLOCK = 256
MOBA_TOPK = 3
MOBA_QCHUNK = 32
SB_QBLOCK = 128
D_FF = 4 * D_MODEL
N_BRANCHES = 3
NORM_EPS = 1e-6
IN_COLS = 3 * MIX_WIDTH + N_BRANCHES * D_MODEL

kernel_name = "hybrid_dilated_moba_stickbreak_block"


def rmsnorm(x, g):
    xf = x.astype(jnp.float32)
    y = xf * lax.rsqrt(jnp.mean(xf * xf, axis=-1, keepdims=True) + NORM_EPS)
    return (y * g.astype(jnp.float32)).astype(x.dtype)


def partial_rope(t, pos):
    inv_freq = jnp.exp(-math.log(ROPE_THETA) * jnp.arange(0, ROPE_DIM, 2, dtype=jnp.float32) / ROPE_DIM)
    ang = pos[:, None] * inv_freq[None, :]
    cos, sin = jnp.cos(ang), jnp.sin(ang)
    half = ROPE_DIM // 2
    x1, x2, rest = t[..., :half], t[..., half:ROPE_DIM], t[..., ROPE_DIM:]
    return jnp.concatenate([x1 * cos - x2 * sin, x2 * cos + x1 * sin, rest], axis=-1)


def dilated_window_attention(q, k, v, window, dilation):
    B, H, S, Dh = q.shape
    span = window // dilation
    blk = span
    unit = dilation * blk
    L = -(-S // unit) * unit
    nb = L // unit
    pad = ((0, 0), (0, 0), (0, L - S), (0, 0))

    def split(t):
        t = jnp.pad(t, pad).reshape(B, H, L // dilation, dilation, Dh)
        return jnp.swapaxes(t, 2, 3).reshape(B, H, dilation, nb, blk, Dh)

    def with_prev(t):
        prev = jnp.concatenate([jnp.zeros_like(t[:, :, :, :1]), t[:, :, :, :-1]], axis=3)
        return jnp.concatenate([prev, t], axis=4)

    qb = split(q)
    kc, vc = with_prev(split(k)), with_prev(split(v))
    s = jnp.einsum('bhrnqd,bhrnkd->bhrnqk', qb, kc) * (Dh ** -0.5)
    qi = jnp.arange(blk)[:, None]
    kj = jnp.arange(2 * blk)[None, :]
    dist = qi + blk - kj
    band = (dist >= 0) & (dist <= span)
    first = (jnp.arange(nb) == 0)[:, None, None]
    mask = band[None] & ~(first & (kj < blk)[None])
    s = jnp.where(mask, s, -jnp.inf)
    m = jnp.max(s, axis=-1, keepdims=True)
    p = jnp.exp(s - m)
    l = jnp.sum(p, axis=-1, keepdims=True)
    o = jnp.einsum('bhrnqk,bhrnkd->bhrnqd', p, vc) / l
    lse = (m + jnp.log(l))[..., 0]

    def merge(t):
        tail = t.shape[5:]
        t = t.reshape((B, H, dilation, L // dilation) + tail)
        return jnp.swapaxes(t, 2, 3).reshape((B, H, L) + tail)[:, :, :S]

    return merge(o), merge(lse)


def dilated_mixture(q, k, v):
    outs, lses = [], []
    for g, (w, d) in enumerate(DIL_GROUPS):
        sl = slice(g * HEADS_PER_DIL_GROUP, (g + 1) * HEADS_PER_DIL_GROUP)
        o, lse = dilated_window_attention(q[:, sl], k[:, sl], v[:, sl], w, d)
        outs.append(o)
        lses.append(lse)
    o = jnp.stack(outs, axis=0)
    wgt = jax.nn.softmax(jnp.stack(lses, axis=0), axis=0)
    return jnp.sum(wgt[..., None] * o, axis=0)


def moba_attention(q, k, v):
    B, H, S, Dh = q.shape
    L = -(-S // MOBA_BLOCK) * MOBA_BLOCK
    nblk = L // MOBA_BLOCK
    topk = min(MOBA_TOPK, nblk)
    pad = ((0, 0), (0, 0), (0, L - S), (0, 0))
    qp, kp, vp = jnp.pad(q, pad), jnp.pad(k, pad), jnp.pad(v, pad)
    kblocks = kp.reshape(B, H, nblk, MOBA_BLOCK, Dh)
    vblocks = vp.reshape(B, H, nblk, MOBA_BLOCK, Dh)
    kmean = jnp.mean(kblocks, axis=3)
    scale = Dh ** -0.5
    gather = jax.vmap(jax.vmap(lambda blocks, idx: blocks[idx]))

    def chunk(c):
        start = c * MOBA_QCHUNK
        qc = lax.dynamic_slice_in_dim(qp, start, MOBA_QCHUNK, axis=2)
        qpos = start + jnp.arange(MOBA_QCHUNK)
        own = start // MOBA_BLOCK
        gate = jnp.einsum('bhqd,bhnd->bhqn', qc, kmean)
        gate = jnp.where((jnp.arange(nblk) < own)[None, None, None, :], gate, -jnp.inf)
        _, idx = lax.top_k(gate, topk)
        valid = idx < own
        ksel = gather(kblocks, idx)
        vsel = gather(vblocks, idx)
        s_sel = jnp.einsum('bhqd,bhqjkd->bhqjk', qc, ksel) * scale
        s_sel = jnp.where(valid[..., None], s_sel, -jnp.inf).reshape(B, H, MOBA_QCHUNK, topk * MOBA_BLOCK)
        kown = lax.dynamic_index_in_dim(kblocks, own, axis=2, keepdims=False)
        vown = lax.dynamic_index_in_dim(vblocks, own, axis=2, keepdims=False)
        s_own = jnp.einsum('bhqd,bhkd->bhqk', qc, kown) * scale
        kpos = own * MOBA_BLOCK + jnp.arange(MOBA_BLOCK)
        s_own = jnp.where(kpos[None, :] <= qpos[:, None], s_own, -jnp.inf)
        p = jax.nn.softmax(jnp.concatenate([s_own, s_sel], axis=-1), axis=-1)
        p_own = p[..., :MOBA_BLOCK]
        p_sel = p[..., MOBA_BLOCK:].reshape(B, H, MOBA_QCHUNK, topk, MOBA_BLOCK)
        return (jnp.einsum('bhqk,bhkd->bhqd', p_own, vown)
                + jnp.einsum('bhqjk,bhqjkd->bhqd', p_sel, vsel))

    o = lax.map(chunk, jnp.arange(L // MOBA_QCHUNK))
    return jnp.moveaxis(o, 0, 2).reshape(B, H, L, Dh)[:, :, :S]


def stick_breaking_attention(q, k, v):
    B, H, S, Dh = q.shape
    scale = Dh ** -0.5
    outs = []
    for start in range(0, S, SB_QBLOCK):
        end = min(start + SB_QBLOCK, S)
        z = jnp.einsum('bhqd,bhkd->bhqk', q[:, :, start:end], k[:, :, :end]) * scale
        causal = jnp.arange(end)[None, :] < jnp.arange(start, end)[:, None]
        log_1m = jnp.where(causal, jax.nn.log_sigmoid(-z), 0.0)
        suffix = lax.cumsum(log_1m, axis=3, reverse=True) - log_1m
        a = jnp.where(causal, jnp.exp(jax.nn.log_sigmoid(z) + suffix), 0.0)
        outs.append(jnp.einsum('bhqk,bhkd->bhqd', a, v[:, :, :end]))
    return jnp.concatenate(outs, axis=2)


def hybrid_layer(x, norm_mix, w_in, w_out_a, w_out_b, w_out_c, w_o, norm_mlp, w_up, w_down, pos):
    B, S, D = x.shape
    h = rmsnorm(x, norm_mix)
    proj = h @ w_in

    def heads(t):
        return t.reshape(B, S, N_HEADS, HEAD_DIM).transpose(0, 2, 1, 3).astype(jnp.float32)

    q = heads(proj[..., :MIX_WIDTH])
    k = heads(proj[..., MIX_WIDTH:2 * MIX_WIDTH])
    v = heads(proj[..., 2 * MIX_WIDTH:3 * MIX_WIDTH])
    gates = jax.nn.sigmoid(proj[..., 3 * MIX_WIDTH:]).reshape(B, S, N_BRANCHES, D)

    n_rot = N_HEADS_A + N_HEADS_B
    q = jnp.concatenate([partial_rope(q[:, :n_rot], pos), q[:, n_rot:]], axis=1)
    k = jnp.concatenate([partial_rope(k[:, :n_rot], pos), k[:, n_rot:]], axis=1)

    a_sl = slice(0, N_HEADS_A)
    b_sl = slice(N_HEADS_A, N_HEADS_A + N_HEADS_B)
    c_sl = slice(N_HEADS_A + N_HEADS_B, N_HEADS)
    o_a = dilated_mixture(q[:, a_sl], k[:, a_sl], v[:, a_sl])
    o_b = moba_attention(q[:, b_sl], k[:, b_sl], v[:, b_sl])
    o_c = stick_breaking_attention(q[:, c_sl], k[:, c_sl], v[:, c_sl])

    def flat(o):
        return o.transpose(0, 2, 1, 3).reshape(B, S, -1).astype(x.dtype)

    y_a = flat(o_a) @ w_out_a
    y_b = flat(o_b) @ w_out_b
    y_c = flat(o_c) @ w_out_c
    merged = gates[:, :, 0] * y_a + gates[:, :, 1] * y_b + gates[:, :, 2] * y_c
    x = x + merged @ w_o

    h = rmsnorm(x, norm_mlp)
    return x + jnp.square(jax.nn.relu(h @ w_up)) @ w_down


def setup_inputs(seed: int = 0) -> dict:
    key = jax.random.key(seed)
    ks = jax.random.split(key, 12)
    f32 = jnp.float32

    def nrm(k, shape, fan_in):
        return jax.random.normal(k, shape, f32) * (fan_in ** -0.5)

    x = jax.random.normal(ks[0], (BATCH, SEQ, D_MODEL), f32)
    norm_mix = 1.0 + 0.05 * jax.random.normal(ks[1], (DEPTH, D_MODEL), f32)
    w_in = nrm(ks[2], (DEPTH, D_MODEL, IN_COLS), D_MODEL)
    w_out_a = nrm(ks[3], (DEPTH, WIDTH_A, D_MODEL), WIDTH_A)
    w_out_b = nrm(ks[4], (DEPTH, WIDTH_B, D_MODEL), WIDTH_B)
    w_out_c = nrm(ks[5], (DEPTH, WIDTH_C, D_MODEL), WIDTH_C)
    w_o = nrm(ks[6], (DEPTH, D_MODEL, D_MODEL), D_MODEL)
    norm_mlp = 1.0 + 0.05 * jax.random.normal(ks[7], (DEPTH, D_MODEL), f32)
    w_up = nrm(ks[8], (DEPTH, D_MODEL, D_FF), D_MODEL)
    w_down = nrm(ks[9], (DEPTH, D_FF, D_MODEL), D_FF)
    norm_final = 1.0 + 0.05 * jax.random.normal(ks[10], (D_MODEL,), f32)
    return {"x": x, "norm_mix": norm_mix, "w_in": w_in, "w_out_a": w_out_a,
            "w_out_b": w_out_b, "w_out_c": w_out_c, "w_o": w_o, "norm_mlp": norm_mlp,
            "w_up": w_up, "w_down": w_down, "norm_final": norm_final}


def reference(x, norm_mix, w_in, w_out_a, w_out_b, w_out_c, w_o, norm_mlp, w_up, w_down, norm_final):
    pos = jnp.arange(x.shape[1], dtype=jnp.float32)
    for layer in range(DEPTH):
        x = hybrid_layer(x, norm_mix[layer], w_in[layer], w_out_a[layer], w_out_b[layer],
                         w_out_c[layer], w_o[layer], norm_mlp[layer], w_up[layer],
                         w_down[layer], pos)
    return rmsnorm(x, norm_final)
```

```python
import functools
import math

import jax
import jax.numpy as jnp
from jax import lax
from jax.experimental import pallas as pl
from jax.experimental.pallas import tpu as pltpu

D_MODEL = 1024
HEAD_DIM = 64
DIL_GROUPS = ((128, 1), (512, 4), (2048, 16))
HEADS_PER_DIL_GROUP = 4
N_HEADS_B = 6
N_HEADS_C = 6
N_HEADS = HEADS_PER_DIL_GROUP * len(DIL_GROUPS) + N_HEADS_B + N_HEADS_C
MIX_WIDTH = N_HEADS * HEAD_DIM
WIDTH_A = HEADS_PER_DIL_GROUP * HEAD_DIM
WIDTH_B = N_HEADS_B * HEAD_DIM
WIDTH_C = N_HEADS_C * HEAD_DIM
ROPE_DIM = HEAD_DIM // 4
ROPE_THETA = 500000.0
MOBA_BLOCK = 256
MOBA_TOPK = 3
D_FF = 4 * D_MODEL
N_BRANCHES = 3
NORM_EPS = 1e-6
DIL_SPAN = 128

LANES = 128
VMEM_LIMIT = 56 * 1024 * 1024

BF16 = jnp.bfloat16
F32 = jnp.float32
NEG_INF = float("-inf")


def _dot(a, b):
    return jnp.dot(a, b, preferred_element_type=F32)


def _dot_nt(a, b):
    return lax.dot_general(a, b, (((1,), (1,)), ((), ())), preferred_element_type=F32)


def _head_lane_masks():
    lane = lax.broadcasted_iota(jnp.int32, (1, LANES), 1)
    lo = lane < HEAD_DIM
    return lo, jnp.logical_not(lo)


def _in_proj_kernel(x_ref, g_ref, w_ref, rope_ref, a1_ref, a2_ref, a3_ref, b_ref, c_ref,
                    gate_ref, scr_ref, *, tm):
    x = x_ref[0]
    ms = jnp.mean(x * x, axis=-1, keepdims=True)
    h = (x * lax.rsqrt(ms + NORM_EPS) * g_ref[...]).astype(BF16)
    cos = rope_ref[0]
    sin_up = rope_ref[1]
    sin_dn = rope_ref[2]

    def rope(t):
        return (t * cos + pltpu.roll(t, ROPE_DIM // 2, 1) * sin_up
                + pltpu.roll(t, LANES - ROPE_DIM // 2, 1) * sin_dn)

    scale = HEAD_DIM ** -0.5
    for sec in range(3):
        base = sec * MIX_WIDTH
        for blk in range(MIX_WIDTH // 256):
            t2 = _dot(h, w_ref[:, base + blk * 256: base + (blk + 1) * 256])
            for half in range(2):
                col = blk * 256 + half * LANES
                t = t2[:, half * LANES:(half + 1) * LANES]
                if sec < 2 and col < MIX_WIDTH - WIDTH_C:
                    t = rope(t)
                if sec == 0:
                    t = t * scale
                if col < WIDTH_A:
                    a1_ref[0, :, sec * WIDTH_A + col: sec * WIDTH_A + col + LANES] = t.astype(BF16)
                elif col < 3 * WIDTH_A:
                    scr_ref[...] = t
                    grp = col // WIDTH_A
                    d = DIL_GROUPS[grp][1]
                    out_ref = a2_ref if grp == 1 else a3_ref
                    for r in range(d):
                        c0 = r * 3 * WIDTH_A + sec * WIDTH_A + col % WIDTH_A
                        out_ref[0, :, c0:c0 + LANES] = (
                            scr_ref[pl.ds(r, tm // d, stride=d), :].astype(BF16))
                elif col < 3 * WIDTH_A + WIDTH_B:
                    c0 = sec * WIDTH_B + col - 3 * WIDTH_A
                    b_ref[0, :, c0:c0 + LANES] = t.astype(BF16)
                else:
                    c0 = sec * WIDTH_C + col - 3 * WIDTH_A - WIDTH_B
                    c_ref[0, :, c0:c0 + LANES] = t.astype(BF16)
    gbase = 3 * MIX_WIDTH
    for blk in range(N_BRANCHES * D_MODEL // 256):
        t2 = _dot(h, w_ref[:, gbase + blk * 256: gbase + (blk + 1) * 256])
        gate_ref[0, :, blk * 256:(blk + 1) * 256] = jax.nn.sigmoid(t2).astype(BF16)


def _in_proj(x, g, w_in, rope_tab, *, tm):
    B, S, D = x.shape
    grid = (B, S // tm)
    d2, d3 = DIL_GROUPS[1][1], DIL_GROUPS[2][1]
    out_shape = (
        jax.ShapeDtypeStruct((B, S, 3 * WIDTH_A), BF16),
        jax.ShapeDtypeStruct((B, S // d2, d2 * 3 * WIDTH_A), BF16),
        jax.ShapeDtypeStruct((B, S // d3, d3 * 3 * WIDTH_A), BF16),
        jax.ShapeDtypeStruct((B, S, 3 * WIDTH_B), BF16),
        jax.ShapeDtypeStruct((B, S, 3 * WIDTH_C), BF16),
        jax.ShapeDtypeStruct((B, S, N_BRANCHES * D_MODEL), BF16),
    )
    row = lambda b, i: (b, i, 0)
    out_specs = (
        pl.BlockSpec((1, tm, 3 * WIDTH_A), row),
        pl.BlockSpec((1, tm // d2, d2 * 3 * WIDTH_A), row),
        pl.BlockSpec((1, tm // d3, d3 * 3 * WIDTH_A), row),
        pl.BlockSpec((1, tm, 3 * WIDTH_B), row),
        pl.BlockSpec((1, tm, 3 * WIDTH_C), row),
        pl.BlockSpec((1, tm, N_BRANCHES * D_MODEL), row),
    )
    in_specs = [
        pl.BlockSpec((1, tm, D), row),
        pl.BlockSpec((1, D), lambda b, i: (0, 0)),
        pl.BlockSpec(w_in.shape, lambda b, i: (0, 0), pipeline_mode=pl.Buffered(1)),
        pl.BlockSpec((3, tm, LANES), lambda b, i: (0, i, 0)),
    ]
    return pl.pallas_call(
        functools.partial(_in_proj_kernel, tm=tm),
        grid=grid, in_specs=in_specs, out_specs=out_specs, out_shape=out_shape,
        scratch_shapes=[pltpu.VMEM((tm, LANES), F32)],
        compiler_params=pltpu.CompilerParams(
            dimension_semantics=("parallel", "parallel"), vmem_limit_bytes=VMEM_LIMIT),
        name="in_proj",
    )(x, g, w_in, rope_tab)


def _dil_kernel(q_ref, k_ref, v_ref, o_ref, lse_ref, *, nb):
    lo, hi = _head_lane_masks()
    blk = DIL_SPAN
    qi = lax.broadcasted_iota(jnp.int32, (blk, 2 * blk), 0)
    kj = lax.broadcasted_iota(jnp.int32, (blk, 2 * blk), 1)
    band = jnp.logical_and(kj >= qi, kj <= qi + blk)
    causal = (lax.broadcasted_iota(jnp.int32, (blk, blk), 1)
              <= lax.broadcasted_iota(jnp.int32, (blk, blk), 0))

    def attend(q, kc, vc, mask):
        outs, lses = [], []
        for sel in (lo, hi):
            qh = jnp.where(sel, q, jnp.zeros_like(q))
            s = jnp.where(mask, _dot_nt(qh, kc), NEG_INF)
            m = jnp.max(s, axis=-1, keepdims=True)
            p = jnp.exp(s - m)
            l = jnp.sum(p, axis=-1, keepdims=True)
            outs.append(_dot(p.astype(BF16), vc) / l)
            lses.append(m + jnp.log(l))
        o = jnp.where(lo, outs[0], outs[1])
        lse = jnp.where(lo, lses[0], lses[1])
        return o, lse

    o, lse = attend(q_ref[0, 0:blk, :], k_ref[0, 0:blk, :], v_ref[0, 0:blk, :], causal)
    o_ref[0, 0:blk, :] = o.astype(o_ref.dtype)
    lse_ref[0, 0:blk, :] = lse

    def body(n, carry):
        r0 = pl.multiple_of(n * blk, blk)
        p0 = pl.multiple_of((n - 1) * blk, blk)
        o, lse = attend(q_ref[0, pl.ds(r0, blk), :], k_ref[0, pl.ds(p0, 2 * blk), :],
                        v_ref[0, pl.ds(p0, 2 * blk), :], band)
        o_ref[0, pl.ds(r0, blk), :] = o.astype(o_ref.dtype)
        lse_ref[0, pl.ds(r0, blk), :] = lse
        return carry

    lax.fori_loop(1, nb, body, 0)


def _dilated(arr, d):
    B, Sd, _ = arr.shape
    nb = Sd // DIL_SPAN
    pairs = WIDTH_A // LANES
    per_res = 3 * pairs
    grid = (B, d, pairs)
    blockshape = (1, Sd, LANES)
    in_specs = [
        pl.BlockSpec(blockshape, lambda b, r, j: (b, 0, r * per_res + j)),
        pl.BlockSpec(blockshape, lambda b, r, j: (b, 0, r * per_res + pairs + j)),
        pl.BlockSpec(blockshape, lambda b, r, j: (b, 0, r * per_res + 2 * pairs + j)),
    ]
    out_spec = pl.BlockSpec(blockshape, lambda b, r, j: (b, 0, r * pairs + j))
    return pl.pallas_call(
        functools.partial(_dil_kernel, nb=nb),
        grid=grid, in_specs=in_specs, out_specs=(out_spec, out_spec),
        out_shape=(jax.ShapeDtypeStruct((B, Sd, d * WIDTH_A), BF16),
                   jax.ShapeDtypeStruct((B, Sd, d * WIDTH_A), F32)),
        compiler_params=pltpu.CompilerParams(
            dimension_semantics=("parallel", "parallel", "parallel"),
            vmem_limit_bytes=VMEM_LIMIT),
        name=f"dilated_d{d}",
    )(arr, arr, arr)


def _moba_kernel(q_ref, k_ref, v_ref, o_ref, kmh_ref, kml_ref, m_ref, l_ref, acc_ref, *, nblk):
    i = pl.program_id(2)
    lo, hi = _head_lane_masks()
    bs = MOBA_BLOCK
    S = nblk * bs

    @pl.when(i == 0)
    def _():
        rows = lax.broadcasted_iota(jnp.int32, (nblk, S), 0)
        cols = lax.broadcasted_iota(jnp.int32, (nblk, S), 1)
        member = jnp.logical_and(cols >= rows * bs, cols < (rows + 1) * bs)
        ind = jnp.where(member, 1.0, 0.0).astype(BF16)
        km = _dot(ind, k_ref[0]) * (1.0 / bs)
        kmh = km.astype(BF16)
        kmh_ref[...] = kmh
        kml_ref[...] = (km - kmh.astype(F32)).astype(BF16)

    q = q_ref[0]
    own0 = pl.multiple_of(i * bs, bs)
    k_own = k_ref[0, pl.ds(own0, bs), :]
    v_own = v_ref[0, pl.ds(own0, bs), :]
    qi = lax.broadcasted_iota(jnp.int32, (bs, bs), 0)
    kj = lax.broadcasted_iota(jnp.int32, (bs, bs), 1)
    causal = kj <= qi
    col = lax.broadcasted_iota(jnp.int32, (bs, nblk), 1)
    past = col < i

    biases = []
    for h, sel in enumerate((lo, hi)):
        qh = jnp.where(sel, q, jnp.zeros_like(q))
        g = _dot_nt(qh, kmh_ref[...]) + _dot_nt(qh, kml_ref[...])
        g = jnp.where(past, g, NEG_INF)
        cnt = jnp.zeros((bs, nblk), jnp.int32)
        for mblk in range(nblk):
            gm = g[:, mblk:mblk + 1]
            beats = jnp.logical_or(gm > g, jnp.logical_and(gm == g, mblk < col))
            cnt = cnt + beats.astype(jnp.int32)
        chosen = jnp.logical_and(cnt < MOBA_TOPK, past)
        biases.append(jnp.where(chosen, 0.0, NEG_INF))
        s = jnp.where(causal, _dot_nt(qh, k_own), NEG_INF)
        m = jnp.max(s, axis=-1, keepdims=True)
        p = jnp.exp(s - m)
        m_ref[h] = m
        l_ref[h] = jnp.sum(p, axis=-1, keepdims=True)
        acc_ref[h] = _dot(p.astype(BF16), v_own)

    def body(n, carry):
        n0 = pl.multiple_of(n * bs, bs)
        kb = k_ref[0, pl.ds(n0, bs), :]
        vb = v_ref[0, pl.ds(n0, bs), :]
        for h, sel in enumerate((lo, hi)):
            qh = jnp.where(sel, q, jnp.zeros_like(q))
            bias = jnp.max(jnp.where(col == n, biases[h], NEG_INF), axis=-1, keepdims=True)
            s = _dot_nt(qh, kb) + bias
            m_old = m_ref[h]
            m_new = jnp.maximum(m_old, jnp.max(s, axis=-1, keepdims=True))
            alpha = jnp.exp(m_old - m_new)
            p = jnp.exp(s - m_new)
            m_ref[h] = m_new
            l_ref[h] = alpha * l_ref[h] + jnp.sum(p, axis=-1, keepdims=True)
            acc_ref[h] = alpha * acc_ref[h] + _dot(p.astype(BF16), vb)
        return carry

    lax.fori_loop(0, i, body, 0)
    o0 = acc_ref[0] / l_ref[0]
    o1 = acc_ref[1] / l_ref[1]
    o_ref[0] = jnp.where(lo, o0, o1).astype(o_ref.dtype)


def _moba(arr):
    B, S, _ = arr.shape
    nblk = S // MOBA_BLOCK
    pairs = WIDTH_B // LANES
    grid = (B, pairs, nblk)
    full = (1, S, LANES)
    in_specs = [
        pl.BlockSpec((1, MOBA_BLOCK, LANES), lambda b, j, i: (b, i, j)),
        pl.BlockSpec(full, lambda b, j, i: (b, 0, pairs + j)),
        pl.BlockSpec(full, lambda b, j, i: (b, 0, 2 * pairs + j)),
    ]
    return pl.pallas_call(
        functools.partial(_moba_kernel, nblk=nblk),
        grid=grid, in_specs=in_specs,
        out_specs=pl.BlockSpec((1, MOBA_BLOCK, LANES), lambda b, j, i: (b, i, j)),
        out_shape=jax.ShapeDtypeStruct((B, S, WIDTH_B), BF16),
        scratch_shapes=[
            pltpu.VMEM((nblk, LANES), BF16), pltpu.VMEM((nblk, LANES), BF16),
            pltpu.VMEM((2, MOBA_BLOCK, 1), F32), pltpu.VMEM((2, MOBA_BLOCK, 1), F32),
            pltpu.VMEM((2, MOBA_BLOCK, LANES), F32),
        ],
        compiler_params=pltpu.CompilerParams(
            dimension_semantics=("parallel", "parallel", "arbitrary"),
            vmem_limit_bytes=VMEM_LIMIT),
        name="moba",
    )(arr, arr, arr)


def _sb_kernel(q_ref, k_ref, v_ref, o_ref, carry_ref, acc_ref, *, tq):
    i = pl.program_id(2)
    lo, hi = _head_lane_masks()
    q = q_ref[0]
    qi = lax.broadcasted_iota(jnp.int32, (tq, tq), 0)
    kj = lax.broadcasted_iota(jnp.int32, (tq, tq), 1)
    strictly_past = kj < qi
    tri = jnp.where(qi >= kj, 1.0, 0.0).astype(BF16)

    carry_ref[...] = jnp.zeros_like(carry_ref)
    acc_ref[...] = jnp.zeros_like(acc_ref)

    def block(n, diag):
        n0 = pl.multiple_of(n * tq, tq)
        kb = k_ref[0, pl.ds(n0, tq), :]
        vb = v_ref[0, pl.ds(n0, tq), :]
        for h, sel in enumerate((lo, hi)):
            qh = jnp.where(sel, q, jnp.zeros_like(q))
            z = _dot_nt(qh, kb)
            log1m = -(jnp.maximum(z, 0.0) + jnp.log(1.0 + jnp.exp(-jnp.abs(z))))
            if diag:
                log1m = jnp.where(strictly_past, log1m, 0.0)
            l_hi = log1m.astype(BF16)
            l_lo = (log1m - l_hi.astype(F32)).astype(BF16)
            cum = _dot(l_hi, tri) + _dot(l_lo, tri)
            a = jnp.exp(z + cum + carry_ref[h])
            if diag:
                a = jnp.where(strictly_past, a, 0.0)
            acc_ref[h] = acc_ref[h] + _dot(a.astype(BF16), vb)
            carry_ref[h] = carry_ref[h] + cum[:, 0:1]

    block(i, True)

    def body(t, c):
        block(i - 1 - t, False)
        return c

    lax.fori_loop(0, i, body, 0)
    o_ref[0] = jnp.where(lo, acc_ref[0], acc_ref[1]).astype(o_ref.dtype)


def _stick_breaking(arr, *, tq):
    B, S, _ = arr.shape
    pairs = WIDTH_C // LANES
    grid = (B, pairs, S // tq)
    full = (1, S, LANES)
    in_specs = [
        pl.BlockSpec((1, tq, LANES), lambda b, j, i: (b, i, j)),
        pl.BlockSpec(full, lambda b, j, i: (b, 0, pairs + j)),
        pl.BlockSpec(full, lambda b, j, i: (b, 0, 2 * pairs + j)),
    ]
    return pl.pallas_call(
        functools.partial(_sb_kernel, tq=tq),
        grid=grid, in_specs=in_specs,
        out_specs=pl.BlockSpec((1, tq, LANES), lambda b, j, i: (b, i, j)),
        out_shape=jax.ShapeDtypeStruct((B, S, WIDTH_C), BF16),
        scratch_shapes=[pltpu.VMEM((2, tq, 1), F32), pltpu.VMEM((2, tq, LANES), F32)],
        compiler_params=pltpu.CompilerParams(
            dimension_semantics=("parallel", "parallel", "parallel"),
            vmem_limit_bytes=VMEM_LIMIT),
        name="stick_breaking",
    )(arr, arr, arr)


def _out_kernel(x_ref, o1_ref, l1_ref, o2_ref, l2_ref, o3_ref, l3_ref, ob_ref, oc_ref, gate_ref,
                wa_ref, wb_ref, wc_ref, wo_ref, y_ref, so2_ref, sl2_ref, so3_ref, sl3_ref, *, tm):
    halves = WIDTH_A // LANES

    def gather_rows(o_ref, l_ref, so_ref, sl_ref, d):
        for r in range(d):
            for half in range(halves):
                c0 = r * WIDTH_A + half * LANES
                so_ref[half, pl.ds(r, tm // d, stride=d), :] = o_ref[0, :, c0:c0 + LANES].astype(F32)
                sl_ref[half, pl.ds(r, tm // d, stride=d), :] = l_ref[0, :, c0:c0 + LANES]
        return (jnp.concatenate([so_ref[half] for half in range(halves)], axis=-1),
                jnp.concatenate([sl_ref[half] for half in range(halves)], axis=-1))

    o1, l1 = o1_ref[0].astype(F32), l1_ref[0]
    o2, l2 = gather_rows(o2_ref, l2_ref, so2_ref, sl2_ref, DIL_GROUPS[1][1])
    o3, l3 = gather_rows(o3_ref, l3_ref, so3_ref, sl3_ref, DIL_GROUPS[2][1])
    mx = jnp.maximum(jnp.maximum(l1, l2), l3)
    e1, e2, e3 = jnp.exp(l1 - mx), jnp.exp(l2 - mx), jnp.exp(l3 - mx)
    oa = (e1 * o1 + e2 * o2 + e3 * o3) / (e1 + e2 + e3)

    ya = _dot(oa.astype(BF16), wa_ref[...])
    yb = _dot(ob_ref[0], wb_ref[...])
    yc = _dot(oc_ref[0], wc_ref[...])
    D = D_MODEL
    merged = (gate_ref[0, :, 0:D].astype(F32) * ya + gate_ref[0, :, D:2 * D].astype(F32) * yb
              + gate_ref[0, :, 2 * D:3 * D].astype(F32) * yc)
    y_ref[0] = x_ref[0] + _dot(merged.astype(BF16), wo_ref[...])


def _out_proj(x, o1, l1, o2, l2, o3, l3, ob, oc, gates, wa, wb, wc, wo, *, tm):
    B, S, D = x.shape
    d2, d3 = DIL_GROUPS[1][1], DIL_GROUPS[2][1]
    row = lambda b, i: (b, i, 0)
    const = lambda b, i: (0, 0)
    in_specs = [
        pl.BlockSpec((1, tm, D), row),
        pl.BlockSpec((1, tm, WIDTH_A), row), pl.BlockSpec((1, tm, WIDTH_A), row),
        pl.BlockSpec((1, tm // d2, d2 * WIDTH_A), row), pl.BlockSpec((1, tm // d2, d2 * WIDTH_A), row),
        pl.BlockSpec((1, tm // d3, d3 * WIDTH_A), row), pl.BlockSpec((1, tm // d3, d3 * WIDTH_A), row),
        pl.BlockSpec((1, tm, WIDTH_B), row), pl.BlockSpec((1, tm, WIDTH_C), row),
        pl.BlockSpec((1, tm, N_BRANCHES * D), row),
        pl.BlockSpec(wa.shape, const), pl.BlockSpec(wb.shape, const),
        pl.BlockSpec(wc.shape, const), pl.BlockSpec(wo.shape, const),
    ]
    return pl.pallas_call(
        functools.partial(_out_kernel, tm=tm),
        grid=(B, S // tm), in_specs=in_specs,
        out_specs=pl.BlockSpec((1, tm, D), row),
        out_shape=jax.ShapeDtypeStruct((B, S, D), F32),
        scratch_shapes=[pltpu.VMEM((WIDTH_A // LANES, tm, LANES), F32)] * 4,
        compiler_params=pltpu.CompilerParams(
            dimension_semantics=("parallel", "parallel"), vmem_limit_bytes=VMEM_LIMIT),
        name="out_proj",
    )(x, o1, l1, o2, l2, o3, l3, ob, oc, gates, wa, wb, wc, wo)


def _mlp_kernel(x_ref, g_ref, wu_ref, wd_ref, gf_ref, y_ref, acc_ref, *, ff_chunk, final_norm):
    x = x_ref[...]
    ms = jnp.mean(x * x, axis=-1, keepdims=True)
    h = (x * lax.rsqrt(ms + NORM_EPS) * g_ref[...]).astype(BF16)
    for c in range(D_FF // ff_chunk):
        u = _dot(h, wu_ref[:, c * ff_chunk:(c + 1) * ff_chunk])
        u = jnp.square(jnp.maximum(u, 0.0)).astype(BF16)
        part = _dot(u, wd_ref[c * ff_chunk:(c + 1) * ff_chunk, :])
        if c == 0:
            acc_ref[...] = part
        else:
            acc_ref[...] += part
    y = x + acc_ref[...]
    if final_norm:
        ms = jnp.mean(y * y, axis=-1, keepdims=True)
        y = y * lax.rsqrt(ms + NORM_EPS) * gf_ref[...]
    y_ref[...] = y


def _mlp(x2d, g, wu, wd, gf, *, tm, final_norm):
    N, D = x2d.shape
    row = lambda i: (i, 0)
    const = lambda i: (0, 0)
    in_specs = [
        pl.BlockSpec((tm, D), row), pl.BlockSpec((1, D), const),
        pl.BlockSpec(wu.shape, const, pipeline_mode=pl.Buffered(1)),
        pl.BlockSpec(wd.shape, const, pipeline_mode=pl.Buffered(1)),
        pl.BlockSpec((1, D), const),
    ]
    return pl.pallas_call(
        functools.partial(_mlp_kernel, ff_chunk=512, final_norm=final_norm),
        grid=(N // tm,), in_specs=in_specs,
        out_specs=pl.BlockSpec((tm, D), row),
        out_shape=jax.ShapeDtypeStruct((N, D), F32),
        scratch_shapes=[pltpu.VMEM((tm, D), F32)],
        compiler_params=pltpu.CompilerParams(
            dimension_semantics=("parallel",), vmem_limit_bytes=VMEM_LIMIT),
        name="mlp",
    )(x2d, g, wu, wd, gf)


def _rope_tables(S):
    half = ROPE_DIM // 2
    inv_freq = jnp.exp(-math.log(ROPE_THETA) * jnp.arange(0, ROPE_DIM, 2, dtype=F32) / ROPE_DIM)
    ang = jnp.arange(S, dtype=F32)[:, None] * inv_freq[None, :]
    cos, sin = jnp.cos(ang), jnp.sin(ang)
    zeros = jnp.zeros((S, HEAD_DIM - ROPE_DIM), F32)
    zhalf = jnp.zeros((S, half), F32)
    cos_t = jnp.concatenate([cos, cos, jnp.ones_like(zeros)], axis=-1)
    up_t = jnp.concatenate([zhalf, sin, zeros], axis=-1)
    dn_t = jnp.concatenate([-sin, zhalf, zeros], axis=-1)
    tab = jnp.stack([cos_t, up_t, dn_t], axis=0)
    return jnp.concatenate([tab, tab], axis=-1)


def _layer(x, g_mix, w_in, wa, wb, wc, wo, g_mlp, wu, wd, g_final, rope_tab, *, final_norm):
    B, S, D = x.shape
    a1, a2, a3, qkv_b, qkv_c, gates = _in_proj(x, g_mix, w_in, rope_tab, tm=512)
    o1, l1 = _dilated(a1, DIL_GROUPS[0][1])
    o2, l2 = _dilated(a2, DIL_GROUPS[1][1])
    o3, l3 = _dilated(a3, DIL_GROUPS[2][1])
    ob = _moba(qkv_b)
    oc = _stick_breaking(qkv_c, tq=128)
    x = _out_proj(x, o1, l1, o2, l2, o3, l3, ob, oc, gates, wa, wb, wc, wo, tm=512)
    y = _mlp(x.reshape(B * S, D), g_mlp, wu, wd, g_final, tm=512, final_norm=final_norm)
    return y.reshape(B, S, D)


def kernel(x, norm_mix, w_in, w_out_a, w_out_b, w_out_c, w_o, norm_mlp, w_up, w_down, norm_final):
    B, S, D = x.shape
    depth = w_in.shape[0]
    assert D == D_MODEL and S % (DIL_GROUPS[-1][0]) == 0 and S % MOBA_BLOCK == 0
    rope_tab = _rope_tables(S)
    gf = norm_final.reshape(1, D)
    for layer in range(depth):
        x = _layer(
            x, norm_mix[layer].reshape(1, D), w_in[layer].astype(BF16),
            w_out_a[layer].astype(BF16), w_out_b[layer].astype(BF16), w_out_c[layer].astype(BF16),
            w_o[layer].astype(BF16), norm_mlp[layer].reshape(1, D),
            w_up[layer].astype(BF16), w_down[layer].astype(BF16), gf, rope_tab,
            final_norm=(layer == depth - 1))
    return x
```

```python
import functools
import math

import jax
import jax.numpy as jnp
from jax import lax
from jax.experimental import pallas as pl
from jax.experimental.pallas import tpu as pltpu

D_MODEL = 1024
HEAD_DIM = 64
DIL_GROUPS = ((128, 1), (512, 4), (2048, 16))
HEADS_PER_DIL_GROUP = 4
N_HEADS_B = 6
N_HEADS_C = 6
N_HEADS = HEADS_PER_DIL_GROUP * len(DIL_GROUPS) + N_HEADS_B + N_HEADS_C
MIX_WIDTH = N_HEADS * HEAD_DIM
WIDTH_A = HEADS_PER_DIL_GROUP * HEAD_DIM
WIDTH_B = N_HEADS_B * HEAD_DIM
WIDTH_C = N_HEADS_C * HEAD_DIM
ROPE_DIM = HEAD_DIM // 4
ROPE_THETA = 500000.0
MOBA_BLOCK = 256
MOBA_TOPK = 3
D_FF = 4 * D_MODEL
N_BRANCHES = 3
NORM_EPS = 1e-6
DIL_SPAN = 128

LANES = 128
VMEM_LIMIT = 56 * 1024 * 1024

BF16 = jnp.bfloat16
F32 = jnp.float32
NEG_INF = float("-inf")


def _dot(a, b):
    return jnp.dot(a, b, preferred_element_type=F32)


def _dot_nt(a, b):
    return lax.dot_general(a, b, (((1,), (1,)), ((), ())), preferred_element_type=F32)


def _head_lane_masks():
    lane = lax.broadcasted_iota(jnp.int32, (1, LANES), 1)
    lo = lane < HEAD_DIM
    return lo, jnp.logical_not(lo)


def _in_proj_kernel(x_ref, g_ref, w_ref, rope_ref, a1_ref, a2_ref, a3_ref, b_ref, c_ref,
                    gate_ref, scr_ref, *, tm):
    x = x_ref[0]
    ms = jnp.mean(x * x, axis=-1, keepdims=True)
    h = (x * lax.rsqrt(ms + NORM_EPS) * g_ref[...]).astype(BF16)
    cos = rope_ref[0]
    sin_up = rope_ref[1]
    sin_dn = rope_ref[2]

    def rope(t):
        return (t * cos + pltpu.roll(t, ROPE_DIM // 2, 1) * sin_up
                + pltpu.roll(t, LANES - ROPE_DIM // 2, 1) * sin_dn)

    scale = HEAD_DIM ** -0.5
    for sec in range(3):
        base = sec * MIX_WIDTH
        for blk in range(MIX_WIDTH // 256):
            t2 = _dot(h, w_ref[:, base + blk * 256: base + (blk + 1) * 256])
            for half in range(2):
                col = blk * 256 + half * LANES
                t = t2[:, half * LANES:(half + 1) * LANES]
                if sec < 2 and col < MIX_WIDTH - WIDTH_C:
                    t = rope(t)
                if sec == 0:
                    t = t * scale
                if col < WIDTH_A:
                    a1_ref[0, :, sec * WIDTH_A + col: sec * WIDTH_A + col + LANES] = t.astype(BF16)
                elif col < 3 * WIDTH_A:
                    scr_ref[...] = t
                    grp = col // WIDTH_A
                    d = DIL_GROUPS[grp][1]
                    out_ref = a2_ref if grp == 1 else a3_ref
                    for r in range(d):
                        c0 = r * 3 * WIDTH_A + sec * WIDTH_A + col % WIDTH_A
                        out_ref[0, :, c0:c0 + LANES] = (
                            scr_ref[pl.ds(r, tm // d, stride=d), :].astype(BF16))
                elif col < 3 * WIDTH_A + WIDTH_B:
                    c0 = sec * WIDTH_B + col - 3 * WIDTH_A
                    b_ref[0, :, c0:c0 + LANES] = t.astype(BF16)
                else:
                    c0 = sec * WIDTH_C + col - 3 * WIDTH_A - WIDTH_B
                    c_ref[0, :, c0:c0 + LANES] = t.astype(BF16)
    gbase = 3 * MIX_WIDTH
    for blk in range(N_BRANCHES * D_MODEL // 256):
        t2 = _dot(h, w_ref[:, gbase + blk * 256: gbase + (blk + 1) * 256])
        gate_ref[0, :, blk * 256:(blk + 1) * 256] = jax.nn.sigmoid(t2).astype(BF16)


def _in_proj(x, g, w_in, rope_tab, *, tm):
    B, S, D = x.shape
    grid = (B, S // tm)
    d2, d3 = DIL_GROUPS[1][1], DIL_GROUPS[2][1]
    out_shape = (
        jax.ShapeDtypeStruct((B, S, 3 * WIDTH_A), BF16),
        jax.ShapeDtypeStruct((B, S // d2, d2 * 3 * WIDTH_A), BF16),
        jax.ShapeDtypeStruct((B, S // d3, d3 * 3 * WIDTH_A), BF16),
        jax.ShapeDtypeStruct((B, S, 3 * WIDTH_B), BF16),
        jax.ShapeDtypeStruct((B, S, 3 * WIDTH_C), BF16),
        jax.ShapeDtypeStruct((B, S, N_BRANCHES * D_MODEL), BF16),
    )
    row = lambda b, i: (b, i, 0)
    out_specs = (
        pl.BlockSpec((1, tm, 3 * WIDTH_A), row),
        pl.BlockSpec((1, tm // d2, d2 * 3 * WIDTH_A), row),
        pl.BlockSpec((1, tm // d3, d3 * 3 * WIDTH_A), row),
        pl.BlockSpec((1, tm, 3 * WIDTH_B), row),
        pl.BlockSpec((1, tm, 3 * WIDTH_C), row),
        pl.BlockSpec((1, tm, N_BRANCHES * D_MODEL), row),
    )
    in_specs = [
        pl.BlockSpec((1, tm, D), row),
        pl.BlockSpec((1, D), lambda b, i: (0, 0)),
        pl.BlockSpec(w_in.shape, lambda b, i: (0, 0), pipeline_mode=pl.Buffered(1)),
        pl.BlockSpec((3, tm, LANES), lambda b, i: (0, i, 0)),
    ]
    return pl.pallas_call(
        functools.partial(_in_proj_kernel, tm=tm),
        grid=grid, in_specs=in_specs, out_specs=out_specs, out_shape=out_shape,
        scratch_shapes=[pltpu.VMEM((tm, LANES), F32)],
        compiler_params=pltpu.CompilerParams(
            dimension_semantics=("parallel", "parallel"), vmem_limit_bytes=VMEM_LIMIT),
        name="in_proj",
    )(x, g, w_in, rope_tab)


def _dil_kernel(q_ref, k_ref, v_ref, o_ref, lse_ref, *, nb):
    lo, hi = _head_lane_masks()
    blk = DIL_SPAN
    qi = lax.broadcasted_iota(jnp.int32, (blk, 2 * blk), 0)
    kj = lax.broadcasted_iota(jnp.int32, (blk, 2 * blk), 1)
    band = jnp.logical_and(kj >= qi, kj <= qi + blk)
    causal = (lax.broadcasted_iota(jnp.int32, (blk, blk), 1)
              <= lax.broadcasted_iota(jnp.int32, (blk, blk), 0))

    def attend(q, kc, vc, mask):
        outs, lses = [], []
        for sel in (lo, hi):
            qh = jnp.where(sel, q, jnp.zeros_like(q))
            s = jnp.where(mask, _dot_nt(qh, kc), NEG_INF)
            m = jnp.max(s, axis=-1, keepdims=True)
            p = jnp.exp(s - m)
            l = jnp.sum(p, axis=-1, keepdims=True)
            outs.append(_dot(p.astype(BF16), vc) / l)
            lses.append(m + jnp.log(l))
        o = jnp.where(lo, outs[0], outs[1])
        lse = jnp.where(lo, lses[0], lses[1])
        return o, lse

    o, lse = attend(q_ref[0, 0:blk, :], k_ref[0, 0:blk, :], v_ref[0, 0:blk, :], causal)
    o_ref[0, 0:blk, :] = o.astype(o_ref.dtype)
    lse_ref[0, 0:blk, :] = lse

    def body(n, carry):
        r0 = pl.multiple_of(n * blk, blk)
        p0 = pl.multiple_of((n - 1) * blk, blk)
        o, lse = attend(q_ref[0, pl.ds(r0, blk), :], k_ref[0, pl.ds(p0, 2 * blk), :],
                        v_ref[0, pl.ds(p0, 2 * blk), :], band)
        o_ref[0, pl.ds(r0, blk), :] = o.astype(o_ref.dtype)
        lse_ref[0, pl.ds(r0, blk), :] = lse
        return carry

    lax.fori_loop(1, nb, body, 0)


def _dilated(arr, d):
    B, Sd, _ = arr.shape
    nb = Sd // DIL_SPAN
    pairs = WIDTH_A // LANES
    per_res = 3 * pairs
    grid = (B, d, pairs)
    blockshape = (1, Sd, LANES)
    in_specs = [
        pl.BlockSpec(blockshape, lambda b, r, j: (b, 0, r * per_res + j)),
        pl.BlockSpec(blockshape, lambda b, r, j: (b, 0, r * per_res + pairs + j)),
        pl.BlockSpec(blockshape, lambda b, r, j: (b, 0, r * per_res + 2 * pairs + j)),
    ]
    out_spec = pl.BlockSpec(blockshape, lambda b, r, j: (b, 0, r * pairs + j))
    return pl.pallas_call(
        functools.partial(_dil_kernel, nb=nb),
        grid=grid, in_specs=in_specs, out_specs=(out_spec, out_spec),
        out_shape=(jax.ShapeDtypeStruct((B, Sd, d * WIDTH_A), BF16),
                   jax.ShapeDtypeStruct((B, Sd, d * WIDTH_A), F32)),
        compiler_params=pltpu.CompilerParams(
            dimension_semantics=("parallel", "parallel", "parallel"),
            vmem_limit_bytes=VMEM_LIMIT),
        name=f"dilated_d{d}",
    )(arr, arr, arr)


def _moba_kernel(q_ref, k_ref, v_ref, o_ref, kmh_ref, kml_ref, qa_ref, m_ref, l_ref, acc_ref,
                 *, nblk):
    i = pl.program_id(1)
    lo, hi = _head_lane_masks()
    sels = (lo, hi)
    bs = MOBA_BLOCK
    S = nblk * bs
    pairs = WIDTH_B // LANES
    masked = float(jnp.finfo(BF16).min)
    assert nblk <= HEAD_DIM

    @pl.when(i == 0)
    def _():
        rows = lax.broadcasted_iota(jnp.int32, (nblk, S), 0)
        cols = lax.broadcasted_iota(jnp.int32, (nblk, S), 1)
        member = jnp.logical_and(cols >= rows * bs, cols < (rows + 1) * bs)
        ind = jnp.where(member, 1.0, 0.0).astype(BF16)
        km = _dot(ind, k_ref[0]) * (1.0 / bs)
        kmh = km.astype(BF16)
        kmh_ref[...] = kmh
        kml_ref[...] = (km - kmh.astype(F32)).astype(BF16)

    own0 = pl.multiple_of(i * bs, bs)
    qi = lax.broadcasted_iota(jnp.int32, (bs, bs), 0)
    kj = lax.broadcasted_iota(jnp.int32, (bs, bs), 1)
    causal = kj <= qi
    blk_id = lax.broadcasted_iota(jnp.int32, (nblk, bs), 0)
    past = blk_id < i
    lane = lax.broadcasted_iota(jnp.int32, (1, LANES), 1)
    sel_f = [jnp.where(s_, 1.0, 0.0).astype(BF16) for s_ in sels]
    onehot_off = (HEAD_DIM, 0)

    for pr in range(pairs):
        cs = slice(pr * LANES, (pr + 1) * LANES)
        q = q_ref[0, :, cs]
        k_own = k_ref[0, pl.ds(own0, bs), cs]
        v_own = v_ref[0, pl.ds(own0, bs), cs]
        for h in range(2):
            hd = 2 * pr + h
            qh = q * sel_f[h]
            g = _dot_nt(kmh_ref[:, cs], qh) + _dot_nt(kml_ref[:, cs], qh)
            g = jnp.where(past, g, NEG_INF)
            cnt = jnp.zeros((nblk, bs), jnp.int32)
            for mblk in range(nblk):
                gm = g[mblk:mblk + 1, :]
                beats = jnp.logical_or(gm > g, jnp.logical_and(gm == g, mblk < blk_id))
                cnt = cnt + beats.astype(jnp.int32)
            chosen = jnp.logical_and(cnt < MOBA_TOPK, past)
            bias_t = jnp.where(chosen, 0.0, masked)
            parts = []
            if onehot_off[h]:
                parts.append(jnp.zeros((onehot_off[h], bs), F32))
            parts.append(bias_t)
            parts.append(jnp.zeros((LANES - onehot_off[h] - nblk, bs), F32))
            bias_q = jnp.concatenate(parts, axis=0).T
            qa_ref[hd] = qh + bias_q.astype(BF16)
            s = jnp.where(causal, _dot_nt(qh, k_own), NEG_INF)
            m = jnp.max(s, axis=-1, keepdims=True)
            p = jnp.exp(s - m)
            m_ref[hd] = jnp.broadcast_to(m, (bs, LANES))
            l_ref[hd] = jnp.broadcast_to(jnp.sum(p, axis=-1, keepdims=True), (bs, LANES))
            acc_ref[hd] = _dot(p.astype(BF16), v_own)

    def body(n, carry):
        n0 = pl.multiple_of(n * bs, bs)
        for pr in range(pairs):
            cs = slice(pr * LANES, (pr + 1) * LANES)
            kb = k_ref[0, pl.ds(n0, bs), cs]
            vb = v_ref[0, pl.ds(n0, bs), cs]
            for h in range(2):
                hd = 2 * pr + h
                onehot = jnp.where(lane == onehot_off[h] + n, 1.0, 0.0).astype(BF16)
                s = _dot_nt(qa_ref[hd], kb * sel_f[h] + onehot)
                m_old = m_ref[hd]
                m_new = jnp.maximum(m_old, jnp.max(s, axis=-1, keepdims=True))
                alpha = jnp.exp(m_old - m_new)
                p = jnp.exp(s - jnp.concatenate([m_new] * (bs // LANES), axis=-1))
                m_ref[hd] = m_new
                l_ref[hd] = alpha * l_ref[hd] + jnp.sum(p, axis=-1, keepdims=True)
                acc_ref[hd] = alpha * acc_ref[hd] + _dot(p.astype(BF16), vb)
        return carry

    lax.fori_loop(0, i, body, 0)
    for pr in range(pairs):
        o0 = acc_ref[2 * pr] / l_ref[2 * pr]
        o1 = acc_ref[2 * pr + 1] / l_ref[2 * pr + 1]
        o_ref[0, :, pr * LANES:(pr + 1) * LANES] = jnp.where(lo, o0, o1).astype(o_ref.dtype)


def _moba(arr):
    B, S, _ = arr.shape
    nblk = S // MOBA_BLOCK
    grid = (B, nblk)
    full = (1, S, WIDTH_B)
    tile = (1, MOBA_BLOCK, WIDTH_B)
    in_specs = [
        pl.BlockSpec(tile, lambda b, i: (b, i, 0)),
        pl.BlockSpec(full, lambda b, i: (b, 0, 1)),
        pl.BlockSpec(full, lambda b, i: (b, 0, 2)),
    ]
    stat = pltpu.VMEM((N_HEADS_B, MOBA_BLOCK, LANES), F32)
    return pl.pallas_call(
        functools.partial(_moba_kernel, nblk=nblk),
        grid=grid, in_specs=in_specs,
        out_specs=pl.BlockSpec(tile, lambda b, i: (b, i, 0)),
        out_shape=jax.ShapeDtypeStruct((B, S, WIDTH_B), BF16),
        scratch_shapes=[
            pltpu.VMEM((nblk, WIDTH_B), BF16), pltpu.VMEM((nblk, WIDTH_B), BF16),
            pltpu.VMEM((N_HEADS_B, MOBA_BLOCK, LANES), BF16), stat, stat, stat,
        ],
        compiler_params=pltpu.CompilerParams(
            dimension_semantics=("parallel", "arbitrary"), vmem_limit_bytes=VMEM_LIMIT),
        name="moba",
    )(arr, arr, arr)


SB_DEAD = -106.0


def _sb_kernel(q_ref, k_ref, v_ref, o_ref, carry_ref, acc_ref, *, tq, win):
    i = pl.program_id(1)
    lo, hi = _head_lane_masks()
    sel_f = [jnp.where(s_, 1.0, 0.0).astype(BF16) for s_ in (lo, hi)]
    pairs = WIDTH_C // LANES
    w_blk = jnp.maximum(i - (win // tq - 1), 0)
    w0 = pl.multiple_of(w_blk * tq, tq)
    q_pos = i * tq + lax.broadcasted_iota(jnp.int32, (tq, win), 0)
    k_pos = w0 + lax.broadcasted_iota(jnp.int32, (tq, win), 1)
    strictly_past = k_pos < q_pos
    tj = lax.broadcasted_iota(jnp.int32, (win, win), 0)
    ts = lax.broadcasted_iota(jnp.int32, (win, win), 1)
    tri = jnp.where(tj >= ts, 1.0, 0.0).astype(BF16)

    def log1m_sigmoid(z):
        return -(jnp.maximum(z, 0.0) + jnp.log(1.0 + jnp.exp(-jnp.abs(z))))

    def suffix_sums(log1m, tri_m):
        l_hi = log1m.astype(BF16)
        l_lo = (log1m - l_hi.astype(F32)).astype(BF16)
        return _dot(l_hi, tri_m) + _dot(l_lo, tri_m)

    for pr in range(pairs):
        cs = slice(pr * LANES, (pr + 1) * LANES)
        q = q_ref[0, :, cs]
        kw = k_ref[0, pl.ds(w0, win), cs]
        vw = v_ref[0, pl.ds(w0, win), cs]
        for h in range(2):
            hd = 2 * pr + h
            z = _dot_nt(q * sel_f[h], kw)
            log1m = jnp.where(strictly_past, log1m_sigmoid(z), 0.0)
            cum = suffix_sums(log1m, tri)
            a = jnp.where(strictly_past, jnp.exp(z + cum), 0.0)
            acc_ref[hd] = _dot(a.astype(BF16), vw)
            carry_ref[hd] = jnp.broadcast_to(cum[:, 0:1], (tq, LANES))

    tri_b = tri[:tq, :tq]

    def alive():
        return jnp.max(carry_ref[...]) > SB_DEAD

    def body(state):
        n, _ = state
        n0 = pl.multiple_of(n * tq, tq)
        for pr in range(pairs):
            cs = slice(pr * LANES, (pr + 1) * LANES)
            q = q_ref[0, :, cs]
            kb = k_ref[0, pl.ds(n0, tq), cs]
            vb = v_ref[0, pl.ds(n0, tq), cs]
            for h in range(2):
                hd = 2 * pr + h
                z = _dot_nt(q * sel_f[h], kb)
                cum = suffix_sums(log1m_sigmoid(z), tri_b)
                a = jnp.exp(z + cum + carry_ref[hd])
                acc_ref[hd] = acc_ref[hd] + _dot(a.astype(BF16), vb)
                carry_ref[hd] = carry_ref[hd] + cum[:, 0:1]
        return n - 1, alive()

    lax.while_loop(lambda st: jnp.logical_and(st[0] >= 0, st[1]), body, (w_blk - 1, alive()))
    for pr in range(pairs):
        o_ref[0, :, pr * LANES:(pr + 1) * LANES] = jnp.where(
            lo, acc_ref[2 * pr], acc_ref[2 * pr + 1]).astype(o_ref.dtype)


def _stick_breaking(arr, *, tq, win):
    B, S, _ = arr.shape
    assert win % tq == 0 and S >= win and tq == LANES
    grid = (B, S // tq)
    full = (1, S, WIDTH_C)
    tile = (1, tq, WIDTH_C)
    in_specs = [
        pl.BlockSpec(tile, lambda b, i: (b, i, 0)),
        pl.BlockSpec(full, lambda b, i: (b, 0, 1)),
        pl.BlockSpec(full, lambda b, i: (b, 0, 2)),
    ]
    stat = pltpu.VMEM((N_HEADS_C, tq, LANES), F32)
    return pl.pallas_call(
        functools.partial(_sb_kernel, tq=tq, win=win),
        grid=grid, in_specs=in_specs,
        out_specs=pl.BlockSpec(tile, lambda b, i: (b, i, 0)),
        out_shape=jax.ShapeDtypeStruct((B, S, WIDTH_C), BF16),
        scratch_shapes=[stat, stat],
        compiler_params=pltpu.CompilerParams(
            dimension_semantics=("parallel", "parallel"), vmem_limit_bytes=VMEM_LIMIT),
        name="stick_breaking",
    )(arr, arr, arr)


def _out_kernel(x_ref, o1_ref, l1_ref, o2_ref, l2_ref, o3_ref, l3_ref, ob_ref, oc_ref, gate_ref,
                wa_ref, wb_ref, wc_ref, wo_ref, y_ref, so2_ref, sl2_ref, so3_ref, sl3_ref, *, tm):
    halves = WIDTH_A // LANES

    def gather_rows(o_ref, l_ref, so_ref, sl_ref, d):
        for r in range(d):
            for half in range(halves):
                c0 = r * WIDTH_A + half * LANES
                so_ref[half, pl.ds(r, tm // d, stride=d), :] = o_ref[0, :, c0:c0 + LANES].astype(F32)
                sl_ref[half, pl.ds(r, tm // d, stride=d), :] = l_ref[0, :, c0:c0 + LANES]
        return (jnp.concatenate([so_ref[half] for half in range(halves)], axis=-1),
                jnp.concatenate([sl_ref[half] for half in range(halves)], axis=-1))

    o1, l1 = o1_ref[0].astype(F32), l1_ref[0]
    o2, l2 = gather_rows(o2_ref, l2_ref, so2_ref, sl2_ref, DIL_GROUPS[1][1])
    o3, l3 = gather_rows(o3_ref, l3_ref, so3_ref, sl3_ref, DIL_GROUPS[2][1])
    mx = jnp.maximum(jnp.maximum(l1, l2), l3)
    e1, e2, e3 = jnp.exp(l1 - mx), jnp.exp(l2 - mx), jnp.exp(l3 - mx)
    oa = (e1 * o1 + e2 * o2 + e3 * o3) / (e1 + e2 + e3)

    ya = _dot(oa.astype(BF16), wa_ref[...])
    yb = _dot(ob_ref[0], wb_ref[...])
    yc = _dot(oc_ref[0], wc_ref[...])
    D = D_MODEL
    merged = (gate_ref[0, :, 0:D].astype(F32) * ya + gate_ref[0, :, D:2 * D].astype(F32) * yb
              + gate_ref[0, :, 2 * D:3 * D].astype(F32) * yc)
    y_ref[0] = x_ref[0] + _dot(merged.astype(BF16), wo_ref[...])


def _out_proj(x, o1, l1, o2, l2, o3, l3, ob, oc, gates, wa, wb, wc, wo, *, tm):
    B, S, D = x.shape
    d2, d3 = DIL_GROUPS[1][1], DIL_GROUPS[2][1]
    row = lambda b, i: (b, i, 0)
    const = lambda b, i: (0, 0)
    in_specs = [
        pl.BlockSpec((1, tm, D), row),
        pl.BlockSpec((1, tm, WIDTH_A), row), pl.BlockSpec((1, tm, WIDTH_A), row),
        pl.BlockSpec((1, tm // d2, d2 * WIDTH_A), row), pl.BlockSpec((1, tm // d2, d2 * WIDTH_A), row),
        pl.BlockSpec((1, tm // d3, d3 * WIDTH_A), row), pl.BlockSpec((1, tm // d3, d3 * WIDTH_A), row),
        pl.BlockSpec((1, tm, WIDTH_B), row), pl.BlockSpec((1, tm, WIDTH_C), row),
        pl.BlockSpec((1, tm, N_BRANCHES * D), row),
        pl.BlockSpec(wa.shape, const), pl.BlockSpec(wb.shape, const),
        pl.BlockSpec(wc.shape, const), pl.BlockSpec(wo.shape, const),
    ]
    return pl.pallas_call(
        functools.partial(_out_kernel, tm=tm),
        grid=(B, S // tm), in_specs=in_specs,
        out_specs=pl.BlockSpec((1, tm, D), row),
        out_shape=jax.ShapeDtypeStruct((B, S, D), F32),
        scratch_shapes=[pltpu.VMEM((WIDTH_A // LANES, tm, LANES), F32)] * 4,
        compiler_params=pltpu.CompilerParams(
            dimension_semantics=("parallel", "parallel"), vmem_limit_bytes=VMEM_LIMIT),
        name="out_proj",
    )(x, o1, l1, o2, l2, o3, l3, ob, oc, gates, wa, wb, wc, wo)


def _mlp_kernel(x_ref, g_ref, wu_ref, wd_ref, gf_ref, y_ref, acc_ref, *, ff_chunk, final_norm):
    x = x_ref[...]
    ms = jnp.mean(x * x, axis=-1, keepdims=True)
    h = (x * lax.rsqrt(ms + NORM_EPS) * g_ref[...]).astype(BF16)
    for c in range(D_FF // ff_chunk):
        u = _dot(h, wu_ref[:, c * ff_chunk:(c + 1) * ff_chunk])
        u = jnp.square(jnp.maximum(u, 0.0)).astype(BF16)
        part = _dot(u, wd_ref[c * ff_chunk:(c + 1) * ff_chunk, :])
        if c == 0:
            acc_ref[...] = part
        else:
            acc_ref[...] += part
    y = x + acc_ref[...]
    if final_norm:
        ms = jnp.mean(y * y, axis=-1, keepdims=True)
        y = y * lax.rsqrt(ms + NORM_EPS) * gf_ref[...]
    y_ref[...] = y


def _mlp(x2d, g, wu, wd, gf, *, tm, final_norm):
    N, D = x2d.shape
    row = lambda i: (i, 0)
    const = lambda i: (0, 0)
    in_specs = [
        pl.BlockSpec((tm, D), row), pl.BlockSpec((1, D), const),
        pl.BlockSpec(wu.shape, const, pipeline_mode=pl.Buffered(1)),
        pl.BlockSpec(wd.shape, const, pipeline_mode=pl.Buffered(1)),
        pl.BlockSpec((1, D), const),
    ]
    return pl.pallas_call(
        functools.partial(_mlp_kernel, ff_chunk=512, final_norm=final_norm),
        grid=(N // tm,), in_specs=in_specs,
        out_specs=pl.BlockSpec((tm, D), row),
        out_shape=jax.ShapeDtypeStruct((N, D), F32),
        scratch_shapes=[pltpu.VMEM((tm, D), F32)],
        compiler_params=pltpu.CompilerParams(
            dimension_semantics=("parallel",), vmem_limit_bytes=VMEM_LIMIT),
        name="mlp",
    )(x2d, g, wu, wd, gf)


def _rope_tables(S):
    half = ROPE_DIM // 2
    inv_freq = jnp.exp(-math.log(ROPE_THETA) * jnp.arange(0, ROPE_DIM, 2, dtype=F32) / ROPE_DIM)
    ang = jnp.arange(S, dtype=F32)[:, None] * inv_freq[None, :]
    cos, sin = jnp.cos(ang), jnp.sin(ang)
    zeros = jnp.zeros((S, HEAD_DIM - ROPE_DIM), F32)
    zhalf = jnp.zeros((S, half), F32)
    cos_t = jnp.concatenate([cos, cos, jnp.ones_like(zeros)], axis=-1)
    up_t = jnp.concatenate([zhalf, sin, zeros], axis=-1)
    dn_t = jnp.concatenate([-sin, zhalf, zeros], axis=-1)
    tab = jnp.stack([cos_t, up_t, dn_t], axis=0)
    return jnp.concatenate([tab, tab], axis=-1)


def _layer(x, g_mix, w_in, wa, wb, wc, wo, g_mlp, wu, wd, g_final, rope_tab, *, final_norm):
    B, S, D = x.shape
    a1, a2, a3, qkv_b, qkv_c, gates = _in_proj(x, g_mix, w_in, rope_tab, tm=512)
    o1, l1 = _dilated(a1, DIL_GROUPS[0][1])
    o2, l2 = _dilated(a2, DIL_GROUPS[1][1])
    o3, l3 = _dilated(a3, DIL_GROUPS[2][1])
    ob = _moba(qkv_b)
    oc = _stick_breaking(qkv_c, tq=128, win=384)
    x = _out_proj(x, o1, l1, o2, l2, o3, l3, ob, oc, gates, wa, wb, wc, wo, tm=512)
    y = _mlp(x.reshape(B * S, D), g_mlp, wu, wd, g_final, tm=512, final_norm=final_norm)
    return y.reshape(B, S, D)


def kernel(x, norm_mix, w_in, w_out_a, w_out_b, w_out_c, w_o, norm_mlp, w_up, w_down, norm_final):
    B, S, D = x.shape
    depth = w_in.shape[0]
    assert D == D_MODEL and S % (DIL_GROUPS[-1][0]) == 0 and S % MOBA_BLOCK == 0
    rope_tab = _rope_tables(S)
    gf = norm_final.reshape(1, D)
    for layer in range(depth):
        x = _layer(
            x, norm_mix[layer].reshape(1, D), w_in[layer].astype(BF16),
            w_out_a[layer].astype(BF16), w_out_b[layer].astype(BF16), w_out_c[layer].astype(BF16),
            w_o[layer].astype(BF16), norm_mlp[layer].reshape(1, D),
            w_up[layer].astype(BF16), w_down[layer].astype(BF16), gf, rope_tab,
            final_norm=(layer == depth - 1))
    return x
```

```python
import functools
import math

import jax
import jax.numpy as jnp
from jax import lax
from jax.experimental import pallas as pl
from jax.experimental.pallas import tpu as pltpu

D_MODEL = 1024
HEAD_DIM = 64
DIL_GROUPS = ((128, 1), (512, 4), (2048, 16))
HEADS_PER_DIL_GROUP = 4
N_HEADS_B = 6
N_HEADS_C = 6
N_HEADS = HEADS_PER_DIL_GROUP * len(DIL_GROUPS) + N_HEADS_B + N_HEADS_C
MIX_WIDTH = N_HEADS * HEAD_DIM
WIDTH_A = HEADS_PER_DIL_GROUP * HEAD_DIM
WIDTH_B = N_HEADS_B * HEAD_DIM
WIDTH_C = N_HEADS_C * HEAD_DIM
ROPE_DIM = HEAD_DIM // 4
ROPE_THETA = 500000.0
MOBA_BLOCK = 256
MOBA_TOPK = 3
D_FF = 4 * D_MODEL
N_BRANCHES = 3
NORM_EPS = 1e-6
DIL_SPAN = 128

LANES = 128
VMEM_LIMIT = 56 * 1024 * 1024

BF16 = jnp.bfloat16
F32 = jnp.float32
NEG_INF = float("-inf")


def _dot(a, b):
    return jnp.dot(a, b, preferred_element_type=F32)


def _dot_nt(a, b):
    return lax.dot_general(a, b, (((1,), (1,)), ((), ())), preferred_element_type=F32)


def _head_lane_masks():
    lane = lax.broadcasted_iota(jnp.int32, (1, LANES), 1)
    lo = lane < HEAD_DIM
    return lo, jnp.logical_not(lo)


def _in_proj_kernel(x_ref, g_ref, w_ref, rope_ref, a1_ref, a2_ref, a3_ref, b_ref, c_ref,
                    gate_ref, scr_ref, *, tm):
    x = x_ref[0]
    ms = jnp.mean(x * x, axis=-1, keepdims=True)
    h = (x * lax.rsqrt(ms + NORM_EPS) * g_ref[...]).astype(BF16)
    cos = rope_ref[0]
    sin_up = rope_ref[1]
    sin_dn = rope_ref[2]

    def rope(t):
        return (t * cos + pltpu.roll(t, ROPE_DIM // 2, 1) * sin_up
                + pltpu.roll(t, LANES - ROPE_DIM // 2, 1) * sin_dn)

    scale = HEAD_DIM ** -0.5
    for sec in range(3):
        base = sec * MIX_WIDTH
        for blk in range(MIX_WIDTH // 256):
            t2 = _dot(h, w_ref[:, base + blk * 256: base + (blk + 1) * 256])
            for half in range(2):
                col = blk * 256 + half * LANES
                t = t2[:, half * LANES:(half + 1) * LANES]
                if sec < 2 and col < MIX_WIDTH - WIDTH_C:
                    t = rope(t)
                if sec == 0:
                    t = t * scale
                if col < WIDTH_A:
                    a1_ref[0, :, sec * WIDTH_A + col: sec * WIDTH_A + col + LANES] = t.astype(BF16)
                elif col < 3 * WIDTH_A:
                    scr_ref[...] = t
                    grp = col // WIDTH_A
                    d = DIL_GROUPS[grp][1]
                    out_ref = a2_ref if grp == 1 else a3_ref
                    for r in range(d):
                        c0 = r * 3 * WIDTH_A + sec * WIDTH_A + col % WIDTH_A
                        out_ref[0, :, c0:c0 + LANES] = (
                            scr_ref[pl.ds(r, tm // d, stride=d), :].astype(BF16))
                elif col < 3 * WIDTH_A + WIDTH_B:
                    c0 = sec * WIDTH_B + col - 3 * WIDTH_A
                    b_ref[0, :, c0:c0 + LANES] = t.astype(BF16)
                else:
                    c0 = sec * WIDTH_C + col - 3 * WIDTH_A - WIDTH_B
                    c_ref[0, :, c0:c0 + LANES] = t.astype(BF16)
    gbase = 3 * MIX_WIDTH
    for blk in range(N_BRANCHES * D_MODEL // 256):
        t2 = _dot(h, w_ref[:, gbase + blk * 256: gbase + (blk + 1) * 256])
        gate_ref[0, :, blk * 256:(blk + 1) * 256] = jax.nn.sigmoid(t2).astype(BF16)


def _in_proj(x, g, w_in, rope_tab, *, tm):
    B, S, D = x.shape
    grid = (B, S // tm)
    d2, d3 = DIL_GROUPS[1][1], DIL_GROUPS[2][1]
    out_shape = (
        jax.ShapeDtypeStruct((B, S, 3 * WIDTH_A), BF16),
        jax.ShapeDtypeStruct((B, S // d2, d2 * 3 * WIDTH_A), BF16),
        jax.ShapeDtypeStruct((B, S // d3, d3 * 3 * WIDTH_A), BF16),
        jax.ShapeDtypeStruct((B, S, 3 * WIDTH_B), BF16),
        jax.ShapeDtypeStruct((B, S, 3 * WIDTH_C), BF16),
        jax.ShapeDtypeStruct((B, S, N_BRANCHES * D_MODEL), BF16),
    )
    row = lambda b, i: (b, i, 0)
    out_specs = (
        pl.BlockSpec((1, tm, 3 * WIDTH_A), row),
        pl.BlockSpec((1, tm // d2, d2 * 3 * WIDTH_A), row),
        pl.BlockSpec((1, tm // d3, d3 * 3 * WIDTH_A), row),
        pl.BlockSpec((1, tm, 3 * WIDTH_B), row),
        pl.BlockSpec((1, tm, 3 * WIDTH_C), row),
        pl.BlockSpec((1, tm, N_BRANCHES * D_MODEL), row),
    )
    in_specs = [
        pl.BlockSpec((1, tm, D), row),
        pl.BlockSpec((1, D), lambda b, i: (0, 0)),
        pl.BlockSpec(w_in.shape, lambda b, i: (0, 0), pipeline_mode=pl.Buffered(1)),
        pl.BlockSpec((3, tm, LANES), lambda b, i: (0, i, 0)),
    ]
    return pl.pallas_call(
        functools.partial(_in_proj_kernel, tm=tm),
        grid=grid, in_specs=in_specs, out_specs=out_specs, out_shape=out_shape,
        scratch_shapes=[pltpu.VMEM((tm, LANES), F32)],
        compiler_params=pltpu.CompilerParams(
            dimension_semantics=("parallel", "parallel"), vmem_limit_bytes=VMEM_LIMIT),
        name="in_proj",
    )(x, g, w_in, rope_tab)


def _dil_kernel(q_ref, k_ref, v_ref, o_ref, lse_ref, *, nb):
    lo, hi = _head_lane_masks()
    sel_f = [jnp.where(s_, 1.0, 0.0).astype(BF16) for s_ in (lo, hi)]
    blk = DIL_SPAN
    pairs = WIDTH_A // LANES
    qi = lax.broadcasted_iota(jnp.int32, (2 * blk, 2 * blk), 0) % blk
    kj = lax.broadcasted_iota(jnp.int32, (2 * blk, 2 * blk), 1)
    band = jnp.logical_and(kj >= qi, kj <= qi + blk)
    first = kj <= qi
    ones = jnp.ones((2 * blk, LANES), BF16)

    def attend(r0, p0, mask):
        for pr in range(pairs):
            cs = slice(pr * LANES, (pr + 1) * LANES)
            q = q_ref[0, pl.ds(r0, blk), cs]
            kc = k_ref[0, pl.ds(p0, 2 * blk), cs]
            vc = v_ref[0, pl.ds(p0, 2 * blk), cs]
            qs = jnp.concatenate([q * sel_f[0], q * sel_f[1]], axis=0)
            s = jnp.where(mask, _dot_nt(qs, kc), NEG_INF)
            m = jnp.max(s, axis=-1, keepdims=True)
            p = jnp.exp(s - m).astype(BF16)
            ov = _dot(p, jnp.concatenate([vc, ones], axis=-1))
            l = ov[:, LANES:]
            o = ov[:, :LANES] / l
            lse = m + jnp.log(l)
            o_ref[0, pl.ds(r0, blk), cs] = jnp.where(lo, o[:blk], o[blk:]).astype(o_ref.dtype)
            lse_ref[0, pl.ds(r0, blk), cs] = jnp.where(lo, lse[:blk], lse[blk:])

    attend(0, 0, first)

    def body(t, carry):
        p0 = pl.multiple_of(2 * t * blk, blk)
        r0 = pl.multiple_of((2 * t + 1) * blk, blk)
        r1 = pl.multiple_of((2 * t + 2) * blk, blk)
        attend(r0, p0, band)
        attend(r1, r0, band)
        return carry

    lax.fori_loop(0, (nb - 2) // 2, body, 0)
    attend((nb - 1) * blk, (nb - 2) * blk, band)


def _dilated(arr, d):
    B, Sd, _ = arr.shape
    nb = Sd // DIL_SPAN
    assert nb >= 2 and nb % 2 == 0
    grid = (B, d)
    blockshape = (1, Sd, WIDTH_A)
    in_specs = [
        pl.BlockSpec(blockshape, lambda b, r: (b, 0, 3 * r)),
        pl.BlockSpec(blockshape, lambda b, r: (b, 0, 3 * r + 1)),
        pl.BlockSpec(blockshape, lambda b, r: (b, 0, 3 * r + 2)),
    ]
    out_spec = pl.BlockSpec(blockshape, lambda b, r: (b, 0, r))
    return pl.pallas_call(
        functools.partial(_dil_kernel, nb=nb),
        grid=grid, in_specs=in_specs, out_specs=(out_spec, out_spec),
        out_shape=(jax.ShapeDtypeStruct((B, Sd, d * WIDTH_A), BF16),
                   jax.ShapeDtypeStruct((B, Sd, d * WIDTH_A), F32)),
        compiler_params=pltpu.CompilerParams(
            dimension_semantics=("parallel", "parallel"), vmem_limit_bytes=VMEM_LIMIT),
        name=f"dilated_d{d}",
    )(arr, arr, arr)


def _moba_kernel(q_ref, k_ref, v_ref, o_ref, kmh_ref, kml_ref, qa_ref, m_ref, acc_ref, *, nblk):
    i = pl.program_id(1)
    lo, hi = _head_lane_masks()
    sels = (lo, hi)
    bs = MOBA_BLOCK
    S = nblk * bs
    pairs = WIDTH_B // LANES
    masked = float(jnp.finfo(BF16).min)
    assert nblk < LANES

    @pl.when(i == 0)
    def _():
        rows = lax.broadcasted_iota(jnp.int32, (nblk, S), 0)
        cols = lax.broadcasted_iota(jnp.int32, (nblk, S), 1)
        member = jnp.logical_and(cols >= rows * bs, cols < (rows + 1) * bs)
        ind = jnp.where(member, 1.0, 0.0).astype(BF16)
        km = _dot(ind, k_ref[0]) * (1.0 / bs)
        kmh = km.astype(BF16)
        kmh_ref[...] = kmh
        kml_ref[...] = (km - kmh.astype(F32)).astype(BF16)

    own0 = pl.multiple_of(i * bs, bs)
    qi = lax.broadcasted_iota(jnp.int32, (bs, bs), 0)
    kj = lax.broadcasted_iota(jnp.int32, (bs, bs), 1)
    causal = jnp.concatenate([kj <= qi] * 2, axis=0)
    blk_id = lax.broadcasted_iota(jnp.int32, (nblk, bs), 0)
    past = blk_id < i
    sel_f = [jnp.where(s_, 1.0, 0.0).astype(BF16) for s_ in sels]
    ones = jnp.ones((bs, LANES), BF16)
    ones2 = jnp.ones((2 * bs, LANES), BF16)
    lane_tile = lax.broadcasted_iota(jnp.int32, (2 * bs, LANES), 1)
    row_tile = lax.broadcasted_iota(jnp.int32, (2 * bs, LANES), 0)

    for pr in range(pairs):
        cs = slice(pr * LANES, (pr + 1) * LANES)
        q = q_ref[0, :, cs]
        for h in range(2):
            qh = q * sel_f[h]
            g = _dot_nt(kmh_ref[:, cs], qh) + _dot_nt(kml_ref[:, cs], qh)
            g = jnp.where(past, g, NEG_INF)
            cnt = jnp.zeros((nblk, bs), jnp.int32)
            for mblk in range(nblk):
                gm = g[mblk:mblk + 1, :]
                beats = jnp.logical_or(gm > g, jnp.logical_and(gm == g, mblk < blk_id))
                cnt = cnt + beats.astype(jnp.int32)
            chosen = jnp.logical_and(cnt < MOBA_TOPK, past)
            bias_t = jnp.where(chosen, 0.0, masked)
            bias_q = jnp.concatenate(
                [bias_t, jnp.full((LANES - nblk, bs), masked, F32)], axis=0).T
            qa_ref[pr, h * bs:(h + 1) * bs, :] = jnp.concatenate(
                [qh, bias_q.astype(BF16)], axis=-1)
        k_own = k_ref[0, pl.ds(own0, bs), cs]
        v_own = v_ref[0, pl.ds(own0, bs), cs]
        s = jnp.where(causal, _dot_nt(qa_ref[pr, :, :LANES], k_own), NEG_INF)
        m = jnp.max(s, axis=-1, keepdims=True)
        p = jnp.exp(s - m).astype(BF16)
        m_ref[pr] = jnp.broadcast_to(m, (2 * bs, LANES))
        acc_ref[pr] = _dot(p, jnp.concatenate([v_own, ones], axis=-1))

    def body(t, carry):
        n0 = pl.multiple_of(2 * t * bs, bs)
        second = jnp.where(2 * t + 1 < i, 2 * t + 1, LANES - 1)
        onehot = jnp.where(lane_tile == jnp.where(row_tile < bs, 2 * t, second), 1.0, 0.0)
        onehot = onehot.astype(BF16)
        for pr in range(pairs):
            cs = slice(pr * LANES, (pr + 1) * LANES)
            kb = k_ref[0, pl.ds(n0, 2 * bs), cs]
            vb = v_ref[0, pl.ds(n0, 2 * bs), cs]
            s = _dot_nt(qa_ref[pr], jnp.concatenate([kb, onehot], axis=-1))
            m_old = m_ref[pr]
            m_new = jnp.maximum(m_old, jnp.max(s, axis=-1, keepdims=True))
            alpha = jnp.exp(m_old - m_new)
            p = jnp.exp(s - jnp.concatenate([m_new] * (2 * bs // LANES), axis=-1)).astype(BF16)
            m_ref[pr] = m_new
            acc_ref[pr] = (jnp.concatenate([alpha, alpha], axis=-1) * acc_ref[pr]
                           + _dot(p, jnp.concatenate([vb, ones2], axis=-1)))
        return carry

    lax.fori_loop(0, (i + 1) // 2, body, 0)
    for pr in range(pairs):
        acc = acc_ref[pr]
        o = acc[:, :LANES] / acc[:, LANES:]
        o_ref[0, :, pr * LANES:(pr + 1) * LANES] = jnp.where(lo, o[:bs], o[bs:]).astype(o_ref.dtype)


def _moba(arr):
    B, S, _ = arr.shape
    nblk = S // MOBA_BLOCK
    grid = (B, nblk)
    full = (1, S, WIDTH_B)
    tile = (1, MOBA_BLOCK, WIDTH_B)
    in_specs = [
        pl.BlockSpec(tile, lambda b, i: (b, i, 0)),
        pl.BlockSpec(full, lambda b, i: (b, 0, 1)),
        pl.BlockSpec(full, lambda b, i: (b, 0, 2)),
    ]
    pairs = WIDTH_B // LANES
    return pl.pallas_call(
        functools.partial(_moba_kernel, nblk=nblk),
        grid=grid, in_specs=in_specs,
        out_specs=pl.BlockSpec(tile, lambda b, i: (b, i, 0)),
        out_shape=jax.ShapeDtypeStruct((B, S, WIDTH_B), BF16),
        scratch_shapes=[
            pltpu.VMEM((nblk, WIDTH_B), BF16), pltpu.VMEM((nblk, WIDTH_B), BF16),
            pltpu.VMEM((pairs, 2 * MOBA_BLOCK, 2 * LANES), BF16),
            pltpu.VMEM((pairs, 2 * MOBA_BLOCK, LANES), F32),
            pltpu.VMEM((pairs, 2 * MOBA_BLOCK, 2 * LANES), F32),
        ],
        compiler_params=pltpu.CompilerParams(
            dimension_semantics=("parallel", "arbitrary"), vmem_limit_bytes=VMEM_LIMIT),
        name="moba",
    )(arr, arr, arr)


SB_DEAD = -106.0


def _sb_kernel(q_ref, k_ref, v_ref, o_ref, qs_ref, z_ref, lb_ref, carry_ref, acc_ref):
    i = pl.program_id(1)
    lo, hi = _head_lane_masks()
    sel_f = [jnp.where(s_, 1.0, 0.0).astype(BF16) for s_ in (lo, hi)]
    pairs = WIDTH_C // LANES
    tq = LANES
    win = 3 * tq
    rows = 2 * tq
    w_blk = jnp.maximum(i - 2, 0)
    w0 = pl.multiple_of(w_blk * tq, tq)
    q_pos = i * tq + lax.broadcasted_iota(jnp.int32, (rows, win), 0) % tq
    k_pos = w0 + lax.broadcasted_iota(jnp.int32, (rows, win), 1)
    strictly_past = k_pos < q_pos
    tj = lax.broadcasted_iota(jnp.int32, (2 * tq, 2 * tq), 0)
    ts = lax.broadcasted_iota(jnp.int32, (2 * tq, 2 * tq), 1)
    tri = jnp.where(tj >= ts, 1.0, 0.0).astype(BF16)
    tri_tot = jnp.concatenate([tri[:tq, :tq], jnp.ones((tq, tq), BF16)], axis=-1)

    def log1m_sigmoid(z):
        return -(jnp.maximum(z, 0.0) + jnp.log(1.0 + jnp.exp(-jnp.abs(z))))

    def put_log1m(r, log1m, width):
        lb_ref[r, :width] = log1m.astype(BF16)

    def mm(cols, w):
        return _dot(lb_ref[:, cols], w)

    for pr in range(pairs):
        cs = slice(pr * LANES, (pr + 1) * LANES)
        r = slice(pr * rows, (pr + 1) * rows)
        q = q_ref[0, :, cs]
        qs = jnp.concatenate([q * sel_f[0], q * sel_f[1]], axis=0)
        qs_ref[r, :] = qs
        z = _dot_nt(qs, k_ref[0, pl.ds(w0, win), cs])
        z_ref[r, :] = z
        put_log1m(r, jnp.where(strictly_past, log1m_sigmoid(z), 0.0), win)

    c01 = mm(slice(0, 2 * tq), tri)
    c2 = mm(slice(2 * tq, win), tri_tot)
    tot2 = c2[:, tq:]
    cum = jnp.concatenate([c01[:, :tq] + tot2, c01[:, tq:] + tot2, c2[:, :tq]], axis=-1)
    carry_ref[...] = jnp.broadcast_to(cum[:, 0:1], (pairs * rows, LANES))
    for pr in range(pairs):
        cs = slice(pr * LANES, (pr + 1) * LANES)
        r = slice(pr * rows, (pr + 1) * rows)
        a = jnp.where(strictly_past, jnp.exp(z_ref[r, :] + cum[r, :]), 0.0)
        acc_ref[r, :] = _dot(a.astype(BF16), v_ref[0, pl.ds(w0, win), cs])

    def alive():
        return jnp.max(carry_ref[...]) > SB_DEAD

    def body(state):
        n, _ = state
        n0 = pl.multiple_of(n * tq, tq)
        for pr in range(pairs):
            cs = slice(pr * LANES, (pr + 1) * LANES)
            r = slice(pr * rows, (pr + 1) * rows)
            z = _dot_nt(qs_ref[r, :], k_ref[0, pl.ds(n0, tq), cs])
            z_ref[r, :tq] = z
            put_log1m(r, log1m_sigmoid(z), tq)
        c = mm(slice(0, tq), tri_tot)
        carry = carry_ref[...]
        for pr in range(pairs):
            cs = slice(pr * LANES, (pr + 1) * LANES)
            r = slice(pr * rows, (pr + 1) * rows)
            a = jnp.exp(z_ref[r, :tq] + c[r, :tq] + carry[r, :])
            acc_ref[r, :] = acc_ref[r, :] + _dot(a.astype(BF16), v_ref[0, pl.ds(n0, tq), cs])
        carry_ref[...] = carry + c[:, tq:]
        return n - 1, alive()

    lax.while_loop(lambda st: jnp.logical_and(st[0] >= 0, st[1]), body, (w_blk - 1, alive()))
    for pr in range(pairs):
        o_ref[0, :, pr * LANES:(pr + 1) * LANES] = jnp.where(
            lo, acc_ref[pr * rows:pr * rows + tq, :],
            acc_ref[pr * rows + tq:(pr + 1) * rows, :]).astype(o_ref.dtype)


def _stick_breaking(arr):
    B, S, _ = arr.shape
    tq = LANES
    win = 3 * tq
    assert S >= win and S % tq == 0
    grid = (B, S // tq)
    full = (1, S, WIDTH_C)
    tile = (1, tq, WIDTH_C)
    in_specs = [
        pl.BlockSpec(tile, lambda b, i: (b, i, 0)),
        pl.BlockSpec(full, lambda b, i: (b, 0, 1)),
        pl.BlockSpec(full, lambda b, i: (b, 0, 2)),
    ]
    nrows = N_HEADS_C * tq
    stat = pltpu.VMEM((nrows, LANES), F32)
    return pl.pallas_call(
        _sb_kernel,
        grid=grid, in_specs=in_specs,
        out_specs=pl.BlockSpec(tile, lambda b, i: (b, i, 0)),
        out_shape=jax.ShapeDtypeStruct((B, S, WIDTH_C), BF16),
        scratch_shapes=[
            pltpu.VMEM((nrows, LANES), BF16),
            pltpu.VMEM((nrows, win), F32),
            pltpu.VMEM((nrows, win), BF16),
            stat, stat,
        ],
        compiler_params=pltpu.CompilerParams(
            dimension_semantics=("parallel", "parallel"), vmem_limit_bytes=VMEM_LIMIT),
        name="stick_breaking",
    )(arr, arr, arr)


def _out_kernel(x_ref, o1_ref, l1_ref, o2_ref, l2_ref, o3_ref, l3_ref, ob_ref, oc_ref, gate_ref,
                wa_ref, wb_ref, wc_ref, wo_ref, y_ref, so2_ref, sl2_ref, so3_ref, sl3_ref, *, tm):
    halves = WIDTH_A // LANES

    def gather_rows(o_ref, l_ref, so_ref, sl_ref, d):
        for r in range(d):
            for half in range(halves):
                c0 = r * WIDTH_A + half * LANES
                so_ref[half, pl.ds(r, tm // d, stride=d), :] = o_ref[0, :, c0:c0 + LANES].astype(F32)
                sl_ref[half, pl.ds(r, tm // d, stride=d), :] = l_ref[0, :, c0:c0 + LANES]
        return (jnp.concatenate([so_ref[half] for half in range(halves)], axis=-1),
                jnp.concatenate([sl_ref[half] for half in range(halves)], axis=-1))

    o1, l1 = o1_ref[0].astype(F32), l1_ref[0]
    o2, l2 = gather_rows(o2_ref, l2_ref, so2_ref, sl2_ref, DIL_GROUPS[1][1])
    o3, l3 = gather_rows(o3_ref, l3_ref, so3_ref, sl3_ref, DIL_GROUPS[2][1])
    mx = jnp.maximum(jnp.maximum(l1, l2), l3)
    e1, e2, e3 = jnp.exp(l1 - mx), jnp.exp(l2 - mx), jnp.exp(l3 - mx)
    oa = (e1 * o1 + e2 * o2 + e3 * o3) / (e1 + e2 + e3)

    ya = _dot(oa.astype(BF16), wa_ref[...])
    yb = _dot(ob_ref[0], wb_ref[...])
    yc = _dot(oc_ref[0], wc_ref[...])
    D = D_MODEL
    merged = (gate_ref[0, :, 0:D].astype(F32) * ya + gate_ref[0, :, D:2 * D].astype(F32) * yb
              + gate_ref[0, :, 2 * D:3 * D].astype(F32) * yc)
    y_ref[0] = x_ref[0] + _dot(merged.astype(BF16), wo_ref[...])


def _out_proj(x, o1, l1, o2, l2, o3, l3, ob, oc, gates, wa, wb, wc, wo, *, tm):
    B, S, D = x.shape
    d2, d3 = DIL_GROUPS[1][1], DIL_GROUPS[2][1]
    row = lambda b, i: (b, i, 0)
    const = lambda b, i: (0, 0)
    in_specs = [
        pl.BlockSpec((1, tm, D), row),
        pl.BlockSpec((1, tm, WIDTH_A), row), pl.BlockSpec((1, tm, WIDTH_A), row),
        pl.BlockSpec((1, tm // d2, d2 * WIDTH_A), row), pl.BlockSpec((1, tm // d2, d2 * WIDTH_A), row),
        pl.BlockSpec((1, tm // d3, d3 * WIDTH_A), row), pl.BlockSpec((1, tm // d3, d3 * WIDTH_A), row),
        pl.BlockSpec((1, tm, WIDTH_B), row), pl.BlockSpec((1, tm, WIDTH_C), row),
        pl.BlockSpec((1, tm, N_BRANCHES * D), row),
        pl.BlockSpec(wa.shape, const), pl.BlockSpec(wb.shape, const),
        pl.BlockSpec(wc.shape, const), pl.BlockSpec(wo.shape, const),
    ]
    return pl.pallas_call(
        functools.partial(_out_kernel, tm=tm),
        grid=(B, S // tm), in_specs=in_specs,
        out_specs=pl.BlockSpec((1, tm, D), row),
        out_shape=jax.ShapeDtypeStruct((B, S, D), F32),
        scratch_shapes=[pltpu.VMEM((WIDTH_A // LANES, tm, LANES), F32)] * 4,
        compiler_params=pltpu.CompilerParams(
            dimension_semantics=("parallel", "parallel"), vmem_limit_bytes=VMEM_LIMIT),
        name="out_proj",
    )(x, o1, l1, o2, l2, o3, l3, ob, oc, gates, wa, wb, wc, wo)


def _mlp_kernel(x_ref, g_ref, wu_ref, wd_ref, gf_ref, y_ref, acc_ref, *, ff_chunk, final_norm):
    x = x_ref[...]
    ms = jnp.mean(x * x, axis=-1, keepdims=True)
    h = (x * lax.rsqrt(ms + NORM_EPS) * g_ref[...]).astype(BF16)
    for c in range(D_FF // ff_chunk):
        u = _dot(h, wu_ref[:, c * ff_chunk:(c + 1) * ff_chunk])
        u = jnp.square(jnp.maximum(u, 0.0)).astype(BF16)
        part = _dot(u, wd_ref[c * ff_chunk:(c + 1) * ff_chunk, :])
        if c == 0:
            acc_ref[...] = part
        else:
            acc_ref[...] += part
    y = x + acc_ref[...]
    if final_norm:
        ms = jnp.mean(y * y, axis=-1, keepdims=True)
        y = y * lax.rsqrt(ms + NORM_EPS) * gf_ref[...]
    y_ref[...] = y


def _mlp(x2d, g, wu, wd, gf, *, tm, final_norm):
    N, D = x2d.shape
    row = lambda i: (i, 0)
    const = lambda i: (0, 0)
    in_specs = [
        pl.BlockSpec((tm, D), row), pl.BlockSpec((1, D), const),
        pl.BlockSpec(wu.shape, const, pipeline_mode=pl.Buffered(1)),
        pl.BlockSpec(wd.shape, const, pipeline_mode=pl.Buffered(1)),
        pl.BlockSpec((1, D), const),
    ]
    return pl.pallas_call(
        functools.partial(_mlp_kernel, ff_chunk=512, final_norm=final_norm),
        grid=(N // tm,), in_specs=in_specs,
        out_specs=pl.BlockSpec((tm, D), row),
        out_shape=jax.ShapeDtypeStruct((N, D), F32),
        scratch_shapes=[pltpu.VMEM((tm, D), F32)],
        compiler_params=pltpu.CompilerParams(
            dimension_semantics=("parallel",), vmem_limit_bytes=VMEM_LIMIT),
        name="mlp",
    )(x2d, g, wu, wd, gf)


def _rope_tables(S):
    half = ROPE_DIM // 2
    inv_freq = jnp.exp(-math.log(ROPE_THETA) * jnp.arange(0, ROPE_DIM, 2, dtype=F32) / ROPE_DIM)
    ang = jnp.arange(S, dtype=F32)[:, None] * inv_freq[None, :]
    cos, sin = jnp.cos(ang), jnp.sin(ang)
    zeros = jnp.zeros((S, HEAD_DIM - ROPE_DIM), F32)
    zhalf = jnp.zeros((S, half), F32)
    cos_t = jnp.concatenate([cos, cos, jnp.ones_like(zeros)], axis=-1)
    up_t = jnp.concatenate([zhalf, sin, zeros], axis=-1)
    dn_t = jnp.concatenate([-sin, zhalf, zeros], axis=-1)
    tab = jnp.stack([cos_t, up_t, dn_t], axis=0)
    return jnp.concatenate([tab, tab], axis=-1)


def _layer(x, g_mix, w_in, wa, wb, wc, wo, g_mlp, wu, wd, g_final, rope_tab, *, final_norm):
    B, S, D = x.shape
    a1, a2, a3, qkv_b, qkv_c, gates = _in_proj(x, g_mix, w_in, rope_tab, tm=512)
    o1, l1 = _dilated(a1, DIL_GROUPS[0][1])
    o2, l2 = _dilated(a2, DIL_GROUPS[1][1])
    o3, l3 = _dilated(a3, DIL_GROUPS[2][1])
    ob = _moba(qkv_b)
    oc = _stick_breaking(qkv_c)
    x = _out_proj(x, o1, l1, o2, l2, o3, l3, ob, oc, gates, wa, wb, wc, wo, tm=512)
    y = _mlp(x.reshape(B * S, D), g_mlp, wu, wd, g_final, tm=512, final_norm=final_norm)
    return y.reshape(B, S, D)


def kernel(x, norm_mix, w_in, w_out_a, w_out_b, w_out_c, w_o, norm_mlp, w_up, w_down, norm_final):
    B, S, D = x.shape
    depth = w_in.shape[0]
    assert D == D_MODEL and S % (DIL_GROUPS[-1][0]) == 0 and S % MOBA_BLOCK == 0
    rope_tab = _rope_tables(S)
    gf = norm_final.reshape(1, D)
    for layer in range(depth):
        x = _layer(
            x, norm_mix[layer].reshape(1, D), w_in[layer].astype(BF16),
            w_out_a[layer].astype(BF16), w_out_b[layer].astype(BF16), w_out_c[layer].astype(BF16),
            w_o[layer].astype(BF16), norm_mlp[layer].reshape(1, D),
            w_up[layer].astype(BF16), w_down[layer].astype(BF16), gf, rope_tab,
            final_norm=(layer == depth - 1))
    return x
```

```python
import functools
import math

import jax
import jax.numpy as jnp
from jax import lax
from jax.experimental import pallas as pl
from jax.experimental.pallas import tpu as pltpu

D_MODEL = 1024
HEAD_DIM = 64
DIL_GROUPS = ((128, 1), (512, 4), (2048, 16))
HEADS_PER_DIL_GROUP = 4
N_HEADS_B = 6
N_HEADS_C = 6
N_HEADS = HEADS_PER_DIL_GROUP * len(DIL_GROUPS) + N_HEADS_B + N_HEADS_C
MIX_WIDTH = N_HEADS * HEAD_DIM
WIDTH_A = HEADS_PER_DIL_GROUP * HEAD_DIM
WIDTH_B = N_HEADS_B * HEAD_DIM
WIDTH_C = N_HEADS_C * HEAD_DIM
ROPE_DIM = HEAD_DIM // 4
ROPE_THETA = 500000.0
MOBA_BLOCK = 256
MOBA_TOPK = 3
D_FF = 4 * D_MODEL
N_BRANCHES = 3
NORM_EPS = 1e-6
DIL_SPAN = 128

LANES = 128
VMEM_LIMIT = 56 * 1024 * 1024

BF16 = jnp.bfloat16
F32 = jnp.float32
NEG_INF = float("-inf")


def _dot(a, b):
    return jnp.dot(a, b, preferred_element_type=F32)


def _dot_nt(a, b):
    return lax.dot_general(a, b, (((1,), (1,)), ((), ())), preferred_element_type=F32)


def _head_lane_masks():
    lane = lax.broadcasted_iota(jnp.int32, (1, LANES), 1)
    lo = lane < HEAD_DIM
    return lo, jnp.logical_not(lo)


def _in_proj_kernel(x_ref, g_ref, w_ref, rope_ref, a1_ref, a2_ref, a3_ref, b_ref, c_ref,
                    gate_ref, scr_ref, *, tm):
    x = x_ref[0]
    ms = jnp.mean(x * x, axis=-1, keepdims=True)
    h = (x * lax.rsqrt(ms + NORM_EPS) * g_ref[...]).astype(BF16)
    cos = rope_ref[0]
    sin_up = rope_ref[1]
    sin_dn = rope_ref[2]

    def rope(t):
        return (t * cos + pltpu.roll(t, ROPE_DIM // 2, 1) * sin_up
                + pltpu.roll(t, LANES - ROPE_DIM // 2, 1) * sin_dn)

    scale = HEAD_DIM ** -0.5
    for sec in range(3):
        base = sec * MIX_WIDTH
        for blk in range(MIX_WIDTH // 256):
            t2 = _dot(h, w_ref[:, base + blk * 256: base + (blk + 1) * 256])
            for half in range(2):
                col = blk * 256 + half * LANES
                t = t2[:, half * LANES:(half + 1) * LANES]
                if sec < 2 and col < MIX_WIDTH - WIDTH_C:
                    t = rope(t)
                if sec == 0:
                    t = t * scale
                if col < WIDTH_A:
                    a1_ref[0, :, sec * WIDTH_A + col: sec * WIDTH_A + col + LANES] = t.astype(BF16)
                elif col < 3 * WIDTH_A:
                    scr_ref[...] = t
                    grp = col // WIDTH_A
                    d = DIL_GROUPS[grp][1]
                    out_ref = a2_ref if grp == 1 else a3_ref
                    for r in range(d):
                        c0 = r * 3 * WIDTH_A + sec * WIDTH_A + col % WIDTH_A
                        out_ref[0, :, c0:c0 + LANES] = (
                            scr_ref[pl.ds(r, tm // d, stride=d), :].astype(BF16))
                elif col < 3 * WIDTH_A + WIDTH_B:
                    c0 = sec * WIDTH_B + col - 3 * WIDTH_A
                    b_ref[0, :, c0:c0 + LANES] = t.astype(BF16)
                else:
                    c0 = sec * WIDTH_C + col - 3 * WIDTH_A - WIDTH_B
                    c_ref[0, :, c0:c0 + LANES] = t.astype(BF16)
    gbase = 3 * MIX_WIDTH
    for blk in range(N_BRANCHES * D_MODEL // 256):
        t2 = _dot(h, w_ref[:, gbase + blk * 256: gbase + (blk + 1) * 256])
        gate_ref[0, :, blk * 256:(blk + 1) * 256] = jax.nn.sigmoid(t2).astype(BF16)


def _in_proj(x, g, w_in, rope_tab, *, tm):
    B, S, D = x.shape
    grid = (B, S // tm)
    d2, d3 = DIL_GROUPS[1][1], DIL_GROUPS[2][1]
    out_shape = (
        jax.ShapeDtypeStruct((B, S, 3 * WIDTH_A), BF16),
        jax.ShapeDtypeStruct((B, S // d2, d2 * 3 * WIDTH_A), BF16),
        jax.ShapeDtypeStruct((B, S // d3, d3 * 3 * WIDTH_A), BF16),
        jax.ShapeDtypeStruct((B, S, 3 * WIDTH_B), BF16),
        jax.ShapeDtypeStruct((B, S, 3 * WIDTH_C), BF16),
        jax.ShapeDtypeStruct((B, S, N_BRANCHES * D_MODEL), BF16),
    )
    row = lambda b, i: (b, i, 0)
    out_specs = (
        pl.BlockSpec((1, tm, 3 * WIDTH_A), row),
        pl.BlockSpec((1, tm // d2, d2 * 3 * WIDTH_A), row),
        pl.BlockSpec((1, tm // d3, d3 * 3 * WIDTH_A), row),
        pl.BlockSpec((1, tm, 3 * WIDTH_B), row),
        pl.BlockSpec((1, tm, 3 * WIDTH_C), row),
        pl.BlockSpec((1, tm, N_BRANCHES * D_MODEL), row),
    )
    in_specs = [
        pl.BlockSpec((1, tm, D), row),
        pl.BlockSpec((1, D), lambda b, i: (0, 0)),
        pl.BlockSpec(w_in.shape, lambda b, i: (0, 0), pipeline_mode=pl.Buffered(1)),
        pl.BlockSpec((3, tm, LANES), lambda b, i: (0, i, 0)),
    ]
    return pl.pallas_call(
        functools.partial(_in_proj_kernel, tm=tm),
        grid=grid, in_specs=in_specs, out_specs=out_specs, out_shape=out_shape,
        scratch_shapes=[pltpu.VMEM((tm, LANES), F32)],
        compiler_params=pltpu.CompilerParams(
            dimension_semantics=("parallel", "parallel"), vmem_limit_bytes=VMEM_LIMIT),
        name="in_proj",
    )(x, g, w_in, rope_tab)


def _dil_kernel(qkv_ref, o_ref, lse_ref, *, nb, nres):
    lo, hi = _head_lane_masks()
    sel_f = [jnp.where(s_, 1.0, 0.0).astype(BF16) for s_ in (lo, hi)]
    blk = DIL_SPAN
    pairs = WIDTH_A // LANES
    qi = lax.broadcasted_iota(jnp.int32, (2 * blk, 2 * blk), 0) % blk
    kj = lax.broadcasted_iota(jnp.int32, (2 * blk, 2 * blk), 1)
    band = jnp.logical_and(kj >= qi, kj <= qi + blk)
    first = kj <= qi
    ones = jnp.ones((2 * blk, LANES), BF16)

    def attend(r0, p0, mask):
        for res in range(nres):
            for pr in range(pairs):
                c0 = res * 3 * WIDTH_A + pr * LANES
                q = qkv_ref[0, pl.ds(r0, blk), c0:c0 + LANES]
                kc = qkv_ref[0, pl.ds(p0, 2 * blk), c0 + WIDTH_A:c0 + WIDTH_A + LANES]
                vc = qkv_ref[0, pl.ds(p0, 2 * blk), c0 + 2 * WIDTH_A:c0 + 2 * WIDTH_A + LANES]
                qs = jnp.concatenate([q * sel_f[0], q * sel_f[1]], axis=0)
                s = jnp.where(mask, _dot_nt(qs, kc), NEG_INF)
                m = jnp.max(s, axis=-1, keepdims=True)
                p = jnp.exp(s - m).astype(BF16)
                ov = _dot(p, jnp.concatenate([vc, ones], axis=-1))
                l = ov[:, LANES:]
                o = ov[:, :LANES] / l
                lse = m + jnp.log(l)
                cs = slice(res * WIDTH_A + pr * LANES, res * WIDTH_A + (pr + 1) * LANES)
                o_ref[0, pl.ds(r0, blk), cs] = jnp.where(lo, o[:blk], o[blk:]).astype(o_ref.dtype)
                lse_ref[0, pl.ds(r0, blk), cs] = jnp.where(lo, lse[:blk], lse[blk:])

    attend(0, 0, first)

    def body(t, carry):
        p0 = pl.multiple_of(2 * t * blk, blk)
        r0 = pl.multiple_of((2 * t + 1) * blk, blk)
        r1 = pl.multiple_of((2 * t + 2) * blk, blk)
        attend(r0, p0, band)
        attend(r1, r0, band)
        return carry

    lax.fori_loop(0, (nb - 2) // 2, body, 0)
    attend((nb - 1) * blk, (nb - 2) * blk, band)


def _dilated(arr, d):
    B, Sd, _ = arr.shape
    nb = Sd // DIL_SPAN
    assert nb >= 2 and nb % 2 == 0
    nres = min(d, max(1, 2048 // Sd))
    grid = (B, d // nres)
    in_spec = pl.BlockSpec((1, Sd, nres * 3 * WIDTH_A), lambda b, r: (b, 0, r))
    out_spec = pl.BlockSpec((1, Sd, nres * WIDTH_A), lambda b, r: (b, 0, r))
    return pl.pallas_call(
        functools.partial(_dil_kernel, nb=nb, nres=nres),
        grid=grid, in_specs=[in_spec], out_specs=(out_spec, out_spec),
        out_shape=(jax.ShapeDtypeStruct((B, Sd, d * WIDTH_A), BF16),
                   jax.ShapeDtypeStruct((B, Sd, d * WIDTH_A), F32)),
        compiler_params=pltpu.CompilerParams(
            dimension_semantics=("parallel", "parallel"), vmem_limit_bytes=VMEM_LIMIT),
        name=f"dilated_d{d}",
    )(arr)


MOBA_VROWS = HEAD_DIM + 16


def _moba_kernel(q_ref, k_ref, v_ref, o_ref, kmh_ref, kml_ref, vt_ref, qa_ref, m_ref, acc_ref,
                 s0_ref, s1_ref, s2_ref, p0_ref, p1_ref, p2_ref, *, nblk):
    i = pl.program_id(1)
    lo, hi = _head_lane_masks()
    sels = (lo, hi)
    bs = MOBA_BLOCK
    S = nblk * bs
    pairs = WIDTH_B // LANES
    masked = float(jnp.finfo(BF16).min)
    assert nblk < LANES
    s_refs = (s0_ref, s1_ref, s2_ref)
    p_refs = (p0_ref, p1_ref, p2_ref)
    assert pairs == len(s_refs)

    @pl.when(i == 0)
    def _():
        rows = lax.broadcasted_iota(jnp.int32, (nblk, S), 0)
        cols = lax.broadcasted_iota(jnp.int32, (nblk, S), 1)
        member = jnp.logical_and(cols >= rows * bs, cols < (rows + 1) * bs)
        ind = jnp.where(member, 1.0, 0.0).astype(BF16)
        km = _dot(ind, k_ref[0]) * (1.0 / bs)
        kmh = km.astype(BF16)
        kmh_ref[...] = kmh
        kml_ref[...] = (km - kmh.astype(F32)).astype(BF16)
        vt_ref[:, :, :, HEAD_DIM:, :] = jnp.ones(
            (pairs, 2, nblk, MOBA_VROWS - HEAD_DIM, bs), BF16)

        def transpose_block(n, carry):
            n0 = pl.multiple_of(n * bs, bs)
            for pr in range(pairs):
                vt = v_ref[0, pl.ds(n0, bs), pr * LANES:(pr + 1) * LANES].astype(F32).T
                vt_ref[pr, 0, n, :HEAD_DIM, :] = vt[:HEAD_DIM].astype(BF16)
                vt_ref[pr, 1, n, :HEAD_DIM, :] = vt[HEAD_DIM:].astype(BF16)
            return carry

        lax.fori_loop(0, nblk, transpose_block, 0)

    own0 = pl.multiple_of(i * bs, bs)
    kj = lax.broadcasted_iota(jnp.int32, (bs, bs), 0)
    qi = lax.broadcasted_iota(jnp.int32, (bs, bs), 1)
    causal = jnp.concatenate([kj <= qi] * 2, axis=1)
    blk_id = lax.broadcasted_iota(jnp.int32, (nblk, bs), 0)
    past = blk_id < i
    sel_f = [jnp.where(s_, 1.0, 0.0).astype(BF16) for s_ in sels]
    lane_tile = lax.broadcasted_iota(jnp.int32, (2 * bs, LANES), 1)
    row_tile = lax.broadcasted_iota(jnp.int32, (2 * bs, LANES), 0)

    for pr in range(pairs):
        cs = slice(pr * LANES, (pr + 1) * LANES)
        q = q_ref[0, :, cs]
        for h in range(2):
            qh = q * sel_f[h]
            g = _dot_nt(kmh_ref[:, cs], qh) + _dot_nt(kml_ref[:, cs], qh)
            g = jnp.where(past, g, NEG_INF)
            cnt = jnp.zeros((nblk, bs), jnp.int32)
            for mblk in range(nblk):
                gm = g[mblk:mblk + 1, :]
                beats = jnp.logical_or(gm > g, jnp.logical_and(gm == g, mblk < blk_id))
                cnt = cnt + beats.astype(jnp.int32)
            chosen = jnp.logical_and(cnt < MOBA_TOPK, past)
            bias_t = jnp.where(chosen, 0.0, masked)
            bias_q = jnp.concatenate(
                [bias_t, jnp.full((LANES - nblk, bs), masked, F32)], axis=0).T
            qa_ref[pr, h * bs:(h + 1) * bs, :] = jnp.concatenate(
                [qh, bias_q.astype(BF16)], axis=-1)
        k_own = k_ref[0, pl.ds(own0, bs), cs]
        s = jnp.where(causal, _dot_nt(k_own, qa_ref[pr, :, :LANES]), NEG_INF)
        m = jnp.max(s, axis=0, keepdims=True)
        p = jnp.exp(s - m).astype(BF16)
        m_ref[pr] = m
        for h in range(2):
            acc_ref[pr, h] = _dot(vt_ref[pr, h, i], p[:, h * bs:(h + 1) * bs])

    def body(t, carry):
        n0 = pl.multiple_of(2 * t * bs, bs)
        second = jnp.where(2 * t + 1 < i, 2 * t + 1, LANES - 1)
        onehot = jnp.where(lane_tile == jnp.where(row_tile < bs, 2 * t, second), 1.0, 0.0)
        onehot = onehot.astype(BF16)
        for pr in range(pairs):
            cs = slice(pr * LANES, (pr + 1) * LANES)
            kb = k_ref[0, pl.ds(n0, 2 * bs), cs]
            s_refs[pr][...] = _dot_nt(jnp.concatenate([kb, onehot], axis=-1), qa_ref[pr])
        alphas = []
        for pr in range(pairs):
            s = s_refs[pr][...]
            m_old = m_ref[pr]
            m_new = jnp.maximum(m_old, jnp.max(s, axis=0, keepdims=True))
            alphas.append(jnp.exp(m_old - m_new))
            p_refs[pr][...] = jnp.exp(s - m_new).astype(BF16)
            m_ref[pr] = m_new
        for pr in range(pairs):
            for h in range(2):
                hs = slice(h * bs, (h + 1) * bs)
                pv = (_dot(vt_ref[pr, h, 2 * t], p_refs[pr][:bs, hs])
                      + _dot(vt_ref[pr, h, 2 * t + 1], p_refs[pr][bs:, hs]))
                acc_ref[pr, h] = alphas[pr][:, hs] * acc_ref[pr, h] + pv
        return carry

    lax.fori_loop(0, (i + 1) // 2, body, 0)
    for pr in range(pairs):
        o_t = jnp.concatenate(
            [acc_ref[pr, h, :HEAD_DIM, :] / acc_ref[pr, h, HEAD_DIM:HEAD_DIM + 1, :]
             for h in range(2)], axis=0)
        o_ref[0, :, pr * LANES:(pr + 1) * LANES] = o_t.T.astype(o_ref.dtype)


def _moba(arr):
    B, S, _ = arr.shape
    nblk = S // MOBA_BLOCK
    grid = (B, nblk)
    full = (1, S, WIDTH_B)
    tile = (1, MOBA_BLOCK, WIDTH_B)
    in_specs = [
        pl.BlockSpec(tile, lambda b, i: (b, i, 0)),
        pl.BlockSpec(full, lambda b, i: (b, 0, 1)),
        pl.BlockSpec(full, lambda b, i: (b, 0, 2)),
    ]
    pairs = WIDTH_B // LANES
    return pl.pallas_call(
        functools.partial(_moba_kernel, nblk=nblk),
        grid=grid, in_specs=in_specs,
        out_specs=pl.BlockSpec(tile, lambda b, i: (b, i, 0)),
        out_shape=jax.ShapeDtypeStruct((B, S, WIDTH_B), BF16),
        scratch_shapes=[
            pltpu.VMEM((nblk, WIDTH_B), BF16), pltpu.VMEM((nblk, WIDTH_B), BF16),
            pltpu.VMEM((pairs, 2, nblk, MOBA_VROWS, MOBA_BLOCK), BF16),
            pltpu.VMEM((pairs, 2 * MOBA_BLOCK, 2 * LANES), BF16),
            pltpu.VMEM((pairs, 1, 2 * MOBA_BLOCK), F32),
            pltpu.VMEM((pairs, 2, MOBA_VROWS, MOBA_BLOCK), F32),
        ] + [pltpu.VMEM((2 * MOBA_BLOCK, 2 * MOBA_BLOCK), F32)] * pairs
        + [pltpu.VMEM((2 * MOBA_BLOCK, 2 * MOBA_BLOCK), BF16)] * pairs,
        compiler_params=pltpu.CompilerParams(
            dimension_semantics=("parallel", "arbitrary"), vmem_limit_bytes=VMEM_LIMIT),
        name="moba",
    )(arr, arr, arr)


SB_DEAD = -106.0


def _sb_kernel(q_ref, k_ref, v_ref, o_ref, qs_ref, z_ref, lb_ref, carry_ref, acc_ref):
    i = pl.program_id(1)
    lo, hi = _head_lane_masks()
    sel_f = [jnp.where(s_, 1.0, 0.0).astype(BF16) for s_ in (lo, hi)]
    pairs = WIDTH_C // LANES
    tq = LANES
    win = 3 * tq
    rows = 2 * tq
    w_blk = jnp.maximum(i - 2, 0)
    w0 = pl.multiple_of(w_blk * tq, tq)
    q_pos = i * tq + lax.broadcasted_iota(jnp.int32, (rows, win), 0) % tq
    k_pos = w0 + lax.broadcasted_iota(jnp.int32, (rows, win), 1)
    strictly_past = k_pos < q_pos
    tj = lax.broadcasted_iota(jnp.int32, (2 * tq, 2 * tq), 0)
    ts = lax.broadcasted_iota(jnp.int32, (2 * tq, 2 * tq), 1)
    tri = jnp.where(tj >= ts, 1.0, 0.0).astype(BF16)
    tri_tot = jnp.concatenate([tri[:tq, :tq], jnp.ones((tq, tq), BF16)], axis=-1)

    def log1m_sigmoid(z):
        return -(jnp.maximum(z, 0.0) + jnp.log(1.0 + jnp.exp(-jnp.abs(z))))

    def put_log1m(r, log1m, width):
        lb_ref[r, :width] = log1m.astype(BF16)

    def mm(cols, w):
        return _dot(lb_ref[:, cols], w)

    for pr in range(pairs):
        cs = slice(pr * LANES, (pr + 1) * LANES)
        r = slice(pr * rows, (pr + 1) * rows)
        q = q_ref[0, :, cs]
        qs = jnp.concatenate([q * sel_f[0], q * sel_f[1]], axis=0)
        qs_ref[r, :] = qs
        z = _dot_nt(qs, k_ref[0, pl.ds(w0, win), cs])
        z_ref[r, :] = z
        put_log1m(r, jnp.where(strictly_past, log1m_sigmoid(z), 0.0), win)

    c01 = mm(slice(0, 2 * tq), tri)
    c2 = mm(slice(2 * tq, win), tri_tot)
    tot2 = c2[:, tq:]
    cum = jnp.concatenate([c01[:, :tq] + tot2, c01[:, tq:] + tot2, c2[:, :tq]], axis=-1)
    carry_ref[...] = jnp.broadcast_to(cum[:, 0:1], (pairs * rows, LANES))
    for pr in range(pairs):
        cs = slice(pr * LANES, (pr + 1) * LANES)
        r = slice(pr * rows, (pr + 1) * rows)
        a = jnp.where(strictly_past, jnp.exp(z_ref[r, :] + cum[r, :]), 0.0)
        acc_ref[r, :] = _dot(a.astype(BF16), v_ref[0, pl.ds(w0, win), cs])

    def alive():
        return jnp.max(carry_ref[...]) > SB_DEAD

    def body(state):
        n, _ = state
        n0 = pl.multiple_of(n * tq, tq)
        for pr in range(pairs):
            cs = slice(pr * LANES, (pr + 1) * LANES)
            r = slice(pr * rows, (pr + 1) * rows)
            z = _dot_nt(qs_ref[r, :], k_ref[0, pl.ds(n0, tq), cs])
            z_ref[r, :tq] = z
            put_log1m(r, log1m_sigmoid(z), tq)
        c = mm(slice(0, tq), tri_tot)
        carry = carry_ref[...]
        for pr in range(pairs):
            cs = slice(pr * LANES, (pr + 1) * LANES)
            r = slice(pr * rows, (pr + 1) * rows)
            a = jnp.exp(z_ref[r, :tq] + c[r, :tq] + carry[r, :])
            acc_ref[r, :] = acc_ref[r, :] + _dot(a.astype(BF16), v_ref[0, pl.ds(n0, tq), cs])
        carry_ref[...] = carry + c[:, tq:]
        return n - 1, alive()

    lax.while_loop(lambda st: jnp.logical_and(st[0] >= 0, st[1]), body, (w_blk - 1, alive()))
    for pr in range(pairs):
        o_ref[0, :, pr * LANES:(pr + 1) * LANES] = jnp.where(
            lo, acc_ref[pr * rows:pr * rows + tq, :],
            acc_ref[pr * rows + tq:(pr + 1) * rows, :]).astype(o_ref.dtype)


def _stick_breaking(arr):
    B, S, _ = arr.shape
    tq = LANES
    win = 3 * tq
    assert S >= win and S % tq == 0
    grid = (B, S // tq)
    full = (1, S, WIDTH_C)
    tile = (1, tq, WIDTH_C)
    in_specs = [
        pl.BlockSpec(tile, lambda b, i: (b, i, 0)),
        pl.BlockSpec(full, lambda b, i: (b, 0, 1)),
        pl.BlockSpec(full, lambda b, i: (b, 0, 2)),
    ]
    nrows = N_HEADS_C * tq
    stat = pltpu.VMEM((nrows, LANES), F32)
    return pl.pallas_call(
        _sb_kernel,
        grid=grid, in_specs=in_specs,
        out_specs=pl.BlockSpec(tile, lambda b, i: (b, i, 0)),
        out_shape=jax.ShapeDtypeStruct((B, S, WIDTH_C), BF16),
        scratch_shapes=[
            pltpu.VMEM((nrows, LANES), BF16),
            pltpu.VMEM((nrows, win), F32),
            pltpu.VMEM((nrows, win), BF16),
            stat, stat,
        ],
        compiler_params=pltpu.CompilerParams(
            dimension_semantics=("parallel", "parallel"), vmem_limit_bytes=VMEM_LIMIT),
        name="stick_breaking",
    )(arr, arr, arr)


def _out_mlp_kernel(x_ref, o1_ref, l1_ref, o2_ref, l2_ref, o3_ref, l3_ref, ob_ref, oc_ref,
                    gate_ref, wa_ref, wb_ref, wc_ref, wo_ref, g_ref, wu_ref, wd_ref, gf_ref,
                    y_ref, so2_ref, sl2_ref, so3_ref, sl3_ref, acc_ref,
                    *, tm, ff_chunk, final_norm):
    halves = WIDTH_A // LANES

    def gather_rows(o_ref, l_ref, so_ref, sl_ref, d):
        for r in range(d):
            for half in range(halves):
                c0 = r * WIDTH_A + half * LANES
                so_ref[half, pl.ds(r, tm // d, stride=d), :] = o_ref[0, :, c0:c0 + LANES].astype(F32)
                sl_ref[half, pl.ds(r, tm // d, stride=d), :] = l_ref[0, :, c0:c0 + LANES]
        return (jnp.concatenate([so_ref[half] for half in range(halves)], axis=-1),
                jnp.concatenate([sl_ref[half] for half in range(halves)], axis=-1))

    o1, l1 = o1_ref[0].astype(F32), l1_ref[0]
    o2, l2 = gather_rows(o2_ref, l2_ref, so2_ref, sl2_ref, DIL_GROUPS[1][1])
    o3, l3 = gather_rows(o3_ref, l3_ref, so3_ref, sl3_ref, DIL_GROUPS[2][1])
    mx = jnp.maximum(jnp.maximum(l1, l2), l3)
    e1, e2, e3 = jnp.exp(l1 - mx), jnp.exp(l2 - mx), jnp.exp(l3 - mx)
    oa = (e1 * o1 + e2 * o2 + e3 * o3) / (e1 + e2 + e3)

    ya = _dot(oa.astype(BF16), wa_ref[...])
    yb = _dot(ob_ref[0], wb_ref[...])
    yc = _dot(oc_ref[0], wc_ref[...])
    D = D_MODEL
    merged = (gate_ref[0, :, 0:D].astype(F32) * ya + gate_ref[0, :, D:2 * D].astype(F32) * yb
              + gate_ref[0, :, 2 * D:3 * D].astype(F32) * yc)
    x = x_ref[0] + _dot(merged.astype(BF16), wo_ref[...])

    ms = jnp.mean(x * x, axis=-1, keepdims=True)
    h = (x * lax.rsqrt(ms + NORM_EPS) * g_ref[...]).astype(BF16)
    for c in range(D_FF // ff_chunk):
        u = _dot(h, wu_ref[:, c * ff_chunk:(c + 1) * ff_chunk])
        u = jnp.square(jnp.maximum(u, 0.0)).astype(BF16)
        part = _dot(u, wd_ref[c * ff_chunk:(c + 1) * ff_chunk, :])
        if c == 0:
            acc_ref[...] = x + part
        else:
            acc_ref[...] += part
    y = acc_ref[...]
    if final_norm:
        ms = jnp.mean(y * y, axis=-1, keepdims=True)
        y = y * lax.rsqrt(ms + NORM_EPS) * gf_ref[...]
    y_ref[0] = y


def _out_mlp(x, o1, l1, o2, l2, o3, l3, ob, oc, gates, wa, wb, wc, wo, g_mlp, wu, wd, gf,
             *, tm, final_norm):
    B, S, D = x.shape
    d2, d3 = DIL_GROUPS[1][1], DIL_GROUPS[2][1]
    row = lambda b, i: (b, i, 0)
    const = lambda b, i: (0, 0)
    weight = lambda w: pl.BlockSpec(w.shape, const, pipeline_mode=pl.Buffered(1))
    in_specs = [
        pl.BlockSpec((1, tm, D), row),
        pl.BlockSpec((1, tm, WIDTH_A), row), pl.BlockSpec((1, tm, WIDTH_A), row),
        pl.BlockSpec((1, tm // d2, d2 * WIDTH_A), row), pl.BlockSpec((1, tm // d2, d2 * WIDTH_A), row),
        pl.BlockSpec((1, tm // d3, d3 * WIDTH_A), row), pl.BlockSpec((1, tm // d3, d3 * WIDTH_A), row),
        pl.BlockSpec((1, tm, WIDTH_B), row), pl.BlockSpec((1, tm, WIDTH_C), row),
        pl.BlockSpec((1, tm, N_BRANCHES * D), row),
        weight(wa), weight(wb), weight(wc), weight(wo),
        pl.BlockSpec((1, D), const), weight(wu), weight(wd), pl.BlockSpec((1, D), const),
    ]
    return pl.pallas_call(
        functools.partial(_out_mlp_kernel, tm=tm, ff_chunk=512, final_norm=final_norm),
        grid=(B, S // tm), in_specs=in_specs,
        out_specs=pl.BlockSpec((1, tm, D), row),
        out_shape=jax.ShapeDtypeStruct((B, S, D), F32),
        scratch_shapes=[pltpu.VMEM((WIDTH_A // LANES, tm, LANES), F32)] * 4
        + [pltpu.VMEM((tm, D), F32)],
        compiler_params=pltpu.CompilerParams(
            dimension_semantics=("parallel", "parallel"), vmem_limit_bytes=VMEM_LIMIT),
        name="out_mlp",
    )(x, o1, l1, o2, l2, o3, l3, ob, oc, gates, wa, wb, wc, wo, g_mlp, wu, wd, gf)


def _rope_tables(S):
    half = ROPE_DIM // 2
    inv_freq = jnp.exp(-math.log(ROPE_THETA) * jnp.arange(0, ROPE_DIM, 2, dtype=F32) / ROPE_DIM)
    ang = jnp.arange(S, dtype=F32)[:, None] * inv_freq[None, :]
    cos, sin = jnp.cos(ang), jnp.sin(ang)
    zeros = jnp.zeros((S, HEAD_DIM - ROPE_DIM), F32)
    zhalf = jnp.zeros((S, half), F32)
    cos_t = jnp.concatenate([cos, cos, jnp.ones_like(zeros)], axis=-1)
    up_t = jnp.concatenate([zhalf, sin, zeros], axis=-1)
    dn_t = jnp.concatenate([-sin, zhalf, zeros], axis=-1)
    tab = jnp.stack([cos_t, up_t, dn_t], axis=0)
    return jnp.concatenate([tab, tab], axis=-1)


def _layer(x, g_mix, w_in, wa, wb, wc, wo, g_mlp, wu, wd, g_final, rope_tab, *, final_norm):
    B, S, D = x.shape
    a1, a2, a3, qkv_b, qkv_c, gates = _in_proj(x, g_mix, w_in, rope_tab, tm=512)
    o1, l1 = _dilated(a1, DIL_GROUPS[0][1])
    o2, l2 = _dilated(a2, DIL_GROUPS[1][1])
    o3, l3 = _dilated(a3, DIL_GROUPS[2][1])
    ob = _moba(qkv_b)
    oc = _stick_breaking(qkv_c)
    return _out_mlp(x, o1, l1, o2, l2, o3, l3, ob, oc, gates, wa, wb, wc, wo, g_mlp, wu, wd,
                    g_final, tm=512, final_norm=final_norm)


def kernel(x, norm_mix, w_in, w_out_a, w_out_b, w_out_c, w_o, norm_mlp, w_up, w_down, norm_final):
    B, S, D = x.shape
    depth = w_in.shape[0]
    assert D == D_MODEL and S % (DIL_GROUPS[-1][0]) == 0 and S % MOBA_BLOCK == 0
    rope_tab = _rope_tables(S)
    gf = norm_final.reshape(1, D)
    for layer in range(depth):
        x = _layer(
            x, norm_mix[layer].reshape(1, D), w_in[layer].astype(BF16),
            w_out_a[layer].astype(BF16), w_out_b[layer].astype(BF16), w_out_c[layer].astype(BF16),
            w_o[layer].astype(BF16), norm_mlp[layer].reshape(1, D),
            w_up[layer].astype(BF16), w_down[layer].astype(BF16), gf, rope_tab,
            final_norm=(layer == depth - 1))
    return x
```

```python
import functools
import math

import jax
import jax.numpy as jnp
from jax import lax
from jax.experimental import pallas as pl
from jax.experimental.pallas import tpu as pltpu

D_MODEL = 1024
HEAD_DIM = 64
DIL_GROUPS = ((128, 1), (512, 4), (2048, 16))
HEADS_PER_DIL_GROUP = 4
N_HEADS_B = 6
N_HEADS_C = 6
N_HEADS = HEADS_PER_DIL_GROUP * len(DIL_GROUPS) + N_HEADS_B + N_HEADS_C
MIX_WIDTH = N_HEADS * HEAD_DIM
WIDTH_A = HEADS_PER_DIL_GROUP * HEAD_DIM
WIDTH_B = N_HEADS_B * HEAD_DIM
WIDTH_C = N_HEADS_C * HEAD_DIM
ROPE_DIM = HEAD_DIM // 4
ROPE_THETA = 500000.0
MOBA_BLOCK = 256
MOBA_TOPK = 3
D_FF = 4 * D_MODEL
N_BRANCHES = 3
NORM_EPS = 1e-6
DIL_SPAN = 128

LANES = 128
VMEM_LIMIT = 56 * 1024 * 1024

BF16 = jnp.bfloat16
F32 = jnp.float32
NEG_INF = float("-inf")
LOG2E = math.log2(math.e)
LN2 = math.log(2.0)


def _dot(a, b):
    return jnp.dot(a, b, preferred_element_type=F32)


def _dot_nt(a, b):
    return lax.dot_general(a, b, (((1,), (1,)), ((), ())), preferred_element_type=F32)


def _head_lane_masks():
    lane = lax.broadcasted_iota(jnp.int32, (1, LANES), 1)
    lo = lane < HEAD_DIM
    return lo, jnp.logical_not(lo)


def _in_proj_kernel(x_ref, g_ref, w_ref, rope_ref, a1_ref, a2_ref, a3_ref, b_ref, c_ref,
                    gate_ref, scr_ref, *, tm):
    x = x_ref[0]
    ms = jnp.mean(x * x, axis=-1, keepdims=True)
    h = (x * lax.rsqrt(ms + NORM_EPS) * g_ref[...]).astype(BF16)
    cos = rope_ref[0]
    sin_up = rope_ref[1]
    sin_dn = rope_ref[2]

    def rope(t):
        return (t * cos + pltpu.roll(t, ROPE_DIM // 2, 1) * sin_up
                + pltpu.roll(t, LANES - ROPE_DIM // 2, 1) * sin_dn)

    scale = HEAD_DIM ** -0.5
    for sec in range(3):
        base = sec * MIX_WIDTH
        for blk in range(MIX_WIDTH // 256):
            t2 = _dot(h, w_ref[:, base + blk * 256: base + (blk + 1) * 256])
            for half in range(2):
                col = blk * 256 + half * LANES
                t = t2[:, half * LANES:(half + 1) * LANES]
                if sec < 2 and col < MIX_WIDTH - WIDTH_C:
                    t = rope(t)
                if sec == 0:
                    t = t * (scale * LOG2E)
                if col < WIDTH_A:
                    a1_ref[0, :, sec * WIDTH_A + col: sec * WIDTH_A + col + LANES] = t.astype(BF16)
                elif col < 3 * WIDTH_A:
                    scr_ref[...] = t
                    grp = col // WIDTH_A
                    d = DIL_GROUPS[grp][1]
                    out_ref = a2_ref if grp == 1 else a3_ref
                    for r in range(d):
                        c0 = r * 3 * WIDTH_A + sec * WIDTH_A + col % WIDTH_A
                        out_ref[0, :, c0:c0 + LANES] = (
                            scr_ref[pl.ds(r, tm // d, stride=d), :].astype(BF16))
                elif col < 3 * WIDTH_A + WIDTH_B:
                    c0 = sec * WIDTH_B + col - 3 * WIDTH_A
                    b_ref[0, :, c0:c0 + LANES] = t.astype(BF16)
                else:
                    c0 = sec * WIDTH_C + col - 3 * WIDTH_A - WIDTH_B
                    c_ref[0, :, c0:c0 + LANES] = t.astype(BF16)
    gbase = 3 * MIX_WIDTH
    for blk in range(N_BRANCHES * D_MODEL // 256):
        t2 = _dot(h, w_ref[:, gbase + blk * 256: gbase + (blk + 1) * 256])
        gate_ref[0, :, blk * 256:(blk + 1) * 256] = jax.nn.sigmoid(t2).astype(BF16)


def _in_proj(x, g, w_in, rope_tab, *, tm):
    B, S, D = x.shape
    grid = (B, S // tm)
    d2, d3 = DIL_GROUPS[1][1], DIL_GROUPS[2][1]
    out_shape = (
        jax.ShapeDtypeStruct((B, S, 3 * WIDTH_A), BF16),
        jax.ShapeDtypeStruct((B, S // d2, d2 * 3 * WIDTH_A), BF16),
        jax.ShapeDtypeStruct((B, S // d3, d3 * 3 * WIDTH_A), BF16),
        jax.ShapeDtypeStruct((B, S, 3 * WIDTH_B), BF16),
        jax.ShapeDtypeStruct((B, S, 3 * WIDTH_C), BF16),
        jax.ShapeDtypeStruct((B, S, N_BRANCHES * D_MODEL), BF16),
    )
    row = lambda b, i: (b, i, 0)
    out_specs = (
        pl.BlockSpec((1, tm, 3 * WIDTH_A), row),
        pl.BlockSpec((1, tm // d2, d2 * 3 * WIDTH_A), row),
        pl.BlockSpec((1, tm // d3, d3 * 3 * WIDTH_A), row),
        pl.BlockSpec((1, tm, 3 * WIDTH_B), row),
        pl.BlockSpec((1, tm, 3 * WIDTH_C), row),
        pl.BlockSpec((1, tm, N_BRANCHES * D_MODEL), row),
    )
    in_specs = [
        pl.BlockSpec((1, tm, D), row),
        pl.BlockSpec((1, D), lambda b, i: (0, 0)),
        pl.BlockSpec(w_in.shape, lambda b, i: (0, 0), pipeline_mode=pl.Buffered(1)),
        pl.BlockSpec((3, tm, LANES), lambda b, i: (0, i, 0)),
    ]
    return pl.pallas_call(
        functools.partial(_in_proj_kernel, tm=tm),
        grid=grid, in_specs=in_specs, out_specs=out_specs, out_shape=out_shape,
        scratch_shapes=[pltpu.VMEM((tm, LANES), F32)],
        compiler_params=pltpu.CompilerParams(
            dimension_semantics=("parallel", "parallel"), vmem_limit_bytes=VMEM_LIMIT),
        name="in_proj",
    )(x, g, w_in, rope_tab)


def _dil_kernel(qkv_ref, o_ref, lse_ref, *, nb, nres):
    lo, hi = _head_lane_masks()
    sel_f = [jnp.where(s_, 1.0, 0.0).astype(BF16) for s_ in (lo, hi)]
    blk = DIL_SPAN
    pairs = WIDTH_A // LANES
    qi = lax.broadcasted_iota(jnp.int32, (2 * blk, 2 * blk), 0) % blk
    kj = lax.broadcasted_iota(jnp.int32, (2 * blk, 2 * blk), 1)
    band = jnp.logical_and(kj >= qi, kj <= qi + blk)
    first = kj <= qi
    ones = jnp.ones((2 * blk, LANES), BF16)

    def attend(r0, p0, mask):
        for res in range(nres):
            for pr in range(pairs):
                c0 = res * 3 * WIDTH_A + pr * LANES
                q = qkv_ref[0, pl.ds(r0, blk), c0:c0 + LANES]
                kc = qkv_ref[0, pl.ds(p0, 2 * blk), c0 + WIDTH_A:c0 + WIDTH_A + LANES]
                vc = qkv_ref[0, pl.ds(p0, 2 * blk), c0 + 2 * WIDTH_A:c0 + 2 * WIDTH_A + LANES]
                qs = jnp.concatenate([q * sel_f[0], q * sel_f[1]], axis=0)
                s = jnp.where(mask, _dot_nt(qs, kc), NEG_INF)
                m = jnp.max(s, axis=-1, keepdims=True)
                p = jnp.exp2((s - m).astype(BF16))
                ov = _dot(p, jnp.concatenate([vc, ones], axis=-1))
                l = ov[:, LANES:]
                o = ov[:, :LANES] / l
                lse = m * LN2 + jnp.log(l)
                cs = slice(res * WIDTH_A + pr * LANES, res * WIDTH_A + (pr + 1) * LANES)
                o_ref[0, pl.ds(r0, blk), cs] = jnp.where(lo, o[:blk], o[blk:]).astype(o_ref.dtype)
                lse_ref[0, pl.ds(r0, blk), cs] = jnp.where(lo, lse[:blk], lse[blk:])

    attend(0, 0, first)

    def body(t, carry):
        p0 = pl.multiple_of(2 * t * blk, blk)
        r0 = pl.multiple_of((2 * t + 1) * blk, blk)
        r1 = pl.multiple_of((2 * t + 2) * blk, blk)
        attend(r0, p0, band)
        attend(r1, r0, band)
        return carry

    lax.fori_loop(0, (nb - 2) // 2, body, 0)
    attend((nb - 1) * blk, (nb - 2) * blk, band)


def _dilated(arr, d):
    B, Sd, _ = arr.shape
    nb = Sd // DIL_SPAN
    assert nb >= 2 and nb % 2 == 0
    nres = min(d, max(1, 2048 // Sd))
    grid = (B, d // nres)
    in_spec = pl.BlockSpec((1, Sd, nres * 3 * WIDTH_A), lambda b, r: (b, 0, r))
    out_spec = pl.BlockSpec((1, Sd, nres * WIDTH_A), lambda b, r: (b, 0, r))
    return pl.pallas_call(
        functools.partial(_dil_kernel, nb=nb, nres=nres),
        grid=grid, in_specs=[in_spec], out_specs=(out_spec, out_spec),
        out_shape=(jax.ShapeDtypeStruct((B, Sd, d * WIDTH_A), BF16),
                   jax.ShapeDtypeStruct((B, Sd, d * WIDTH_A), F32)),
        compiler_params=pltpu.CompilerParams(
            dimension_semantics=("parallel", "parallel"), vmem_limit_bytes=VMEM_LIMIT),
        name=f"dilated_d{d}",
    )(arr)


MOBA_VROWS = HEAD_DIM + 16


def _moba_kernel(q_ref, k_ref, v_ref, o_ref, kmh_ref, kml_ref, vt_ref, qa_ref, m_ref, acc_ref,
                 s0_ref, s1_ref, s2_ref, p0_ref, p1_ref, p2_ref, *, nblk):
    i = pl.program_id(1)
    lo, hi = _head_lane_masks()
    sels = (lo, hi)
    bs = MOBA_BLOCK
    S = nblk * bs
    pairs = WIDTH_B // LANES
    masked = float(jnp.finfo(BF16).min)
    assert nblk < LANES
    s_refs = (s0_ref, s1_ref, s2_ref)
    p_refs = (p0_ref, p1_ref, p2_ref)
    assert pairs == len(s_refs)

    @pl.when(i == 0)
    def _():
        rows = lax.broadcasted_iota(jnp.int32, (nblk, S), 0)
        cols = lax.broadcasted_iota(jnp.int32, (nblk, S), 1)
        member = jnp.logical_and(cols >= rows * bs, cols < (rows + 1) * bs)
        ind = jnp.where(member, 1.0, 0.0).astype(BF16)
        km = _dot(ind, k_ref[0]) * (1.0 / bs)
        kmh = km.astype(BF16)
        kmh_ref[...] = kmh
        kml_ref[...] = (km - kmh.astype(F32)).astype(BF16)
        vt_ref[:, :, :, HEAD_DIM:, :] = jnp.ones(
            (pairs, 2, nblk, MOBA_VROWS - HEAD_DIM, bs), BF16)

        def transpose_block(n, carry):
            n0 = pl.multiple_of(n * bs, bs)
            for pr in range(pairs):
                vt = v_ref[0, pl.ds(n0, bs), pr * LANES:(pr + 1) * LANES].astype(F32).T
                vt_ref[pr, 0, n, :HEAD_DIM, :] = vt[:HEAD_DIM].astype(BF16)
                vt_ref[pr, 1, n, :HEAD_DIM, :] = vt[HEAD_DIM:].astype(BF16)
            return carry

        lax.fori_loop(0, nblk, transpose_block, 0)

    own0 = pl.multiple_of(i * bs, bs)
    kj = lax.broadcasted_iota(jnp.int32, (bs, bs), 0)
    qi = lax.broadcasted_iota(jnp.int32, (bs, bs), 1)
    causal = jnp.concatenate([kj <= qi] * 2, axis=1)
    blk_id = lax.broadcasted_iota(jnp.int32, (nblk, bs), 0)
    past = blk_id < i
    sel_f = [jnp.where(s_, 1.0, 0.0).astype(BF16) for s_ in sels]
    lane_tile = lax.broadcasted_iota(jnp.int32, (2 * bs, LANES), 1)
    row_tile = lax.broadcasted_iota(jnp.int32, (2 * bs, LANES), 0)

    for pr in range(pairs):
        cs = slice(pr * LANES, (pr + 1) * LANES)
        q = q_ref[0, :, cs]
        for h in range(2):
            qh = q * sel_f[h]
            g = _dot_nt(kmh_ref[:, cs], qh) + _dot_nt(kml_ref[:, cs], qh)
            g = jnp.where(past, g, NEG_INF)
            cnt = jnp.zeros((nblk, bs), jnp.int32)
            for mblk in range(nblk):
                gm = g[mblk:mblk + 1, :]
                beats = jnp.logical_or(gm > g, jnp.logical_and(gm == g, mblk < blk_id))
                cnt = cnt + beats.astype(jnp.int32)
            chosen = jnp.logical_and(cnt < MOBA_TOPK, past)
            bias_t = jnp.where(chosen, 0.0, masked)
            bias_q = jnp.concatenate(
                [bias_t, jnp.full((LANES - nblk, bs), masked, F32)], axis=0).T
            qa_ref[pr, h * bs:(h + 1) * bs, :] = jnp.concatenate(
                [qh, bias_q.astype(BF16)], axis=-1)
        k_own = k_ref[0, pl.ds(own0, bs), cs]
        s = jnp.where(causal, _dot_nt(k_own, qa_ref[pr, :, :LANES]), NEG_INF)
        m = jnp.max(s, axis=0, keepdims=True)
        p = jnp.exp2((s - m).astype(BF16))
        m_ref[pr] = m
        for h in range(2):
            acc_ref[pr, h] = _dot(vt_ref[pr, h, i], p[:, h * bs:(h + 1) * bs])

    def body(t, carry):
        n0 = pl.multiple_of(2 * t * bs, bs)
        second = jnp.where(2 * t + 1 < i, 2 * t + 1, LANES - 1)
        onehot = jnp.where(lane_tile == jnp.where(row_tile < bs, 2 * t, second), 1.0, 0.0)
        onehot = onehot.astype(BF16)
        for pr in range(pairs):
            cs = slice(pr * LANES, (pr + 1) * LANES)
            kb = k_ref[0, pl.ds(n0, 2 * bs), cs]
            s_refs[pr][...] = _dot_nt(jnp.concatenate([kb, onehot], axis=-1), qa_ref[pr])
        alphas = []
        for pr in range(pairs):
            s = s_refs[pr][...]
            m_old = m_ref[pr]
            m_new = jnp.maximum(m_old, jnp.max(s, axis=0, keepdims=True))
            alphas.append(jnp.exp2(m_old - m_new))
            p_refs[pr][...] = jnp.exp2((s - m_new).astype(BF16))
            m_ref[pr] = m_new
        for pr in range(pairs):
            for h in range(2):
                hs = slice(h * bs, (h + 1) * bs)
                pv = (_dot(vt_ref[pr, h, 2 * t], p_refs[pr][:bs, hs])
                      + _dot(vt_ref[pr, h, 2 * t + 1], p_refs[pr][bs:, hs]))
                acc_ref[pr, h] = alphas[pr][:, hs] * acc_ref[pr, h] + pv
        return carry

    lax.fori_loop(0, (i + 1) // 2, body, 0)
    for pr in range(pairs):
        o_t = jnp.concatenate(
            [acc_ref[pr, h, :HEAD_DIM, :] / acc_ref[pr, h, HEAD_DIM:HEAD_DIM + 1, :]
             for h in range(2)], axis=0)
        o_ref[0, :, pr * LANES:(pr + 1) * LANES] = o_t.T.astype(o_ref.dtype)


def _moba(arr):
    B, S, _ = arr.shape
    nblk = S // MOBA_BLOCK
    grid = (B, nblk)
    full = (1, S, WIDTH_B)
    tile = (1, MOBA_BLOCK, WIDTH_B)
    in_specs = [
        pl.BlockSpec(tile, lambda b, i: (b, i, 0)),
        pl.BlockSpec(full, lambda b, i: (b, 0, 1)),
        pl.BlockSpec(full, lambda b, i: (b, 0, 2)),
    ]
    pairs = WIDTH_B // LANES
    return pl.pallas_call(
        functools.partial(_moba_kernel, nblk=nblk),
        grid=grid, in_specs=in_specs,
        out_specs=pl.BlockSpec(tile, lambda b, i: (b, i, 0)),
        out_shape=jax.ShapeDtypeStruct((B, S, WIDTH_B), BF16),
        scratch_shapes=[
            pltpu.VMEM((nblk, WIDTH_B), BF16), pltpu.VMEM((nblk, WIDTH_B), BF16),
            pltpu.VMEM((pairs, 2, nblk, MOBA_VROWS, MOBA_BLOCK), BF16),
            pltpu.VMEM((pairs, 2 * MOBA_BLOCK, 2 * LANES), BF16),
            pltpu.VMEM((pairs, 1, 2 * MOBA_BLOCK), F32),
            pltpu.VMEM((pairs, 2, MOBA_VROWS, MOBA_BLOCK), F32),
        ] + [pltpu.VMEM((2 * MOBA_BLOCK, 2 * MOBA_BLOCK), F32)] * pairs
        + [pltpu.VMEM((2 * MOBA_BLOCK, 2 * MOBA_BLOCK), BF16)] * pairs,
        compiler_params=pltpu.CompilerParams(
            dimension_semantics=("parallel", "arbitrary"), vmem_limit_bytes=VMEM_LIMIT),
        name="moba",
    )(arr, arr, arr)


SB_DEAD = -153.0


def _sb_kernel(q_ref, k_ref, v_ref, o_ref, qs_ref, z_ref, lb_ref, carry_ref, acc_ref):
    i = pl.program_id(1)
    lo, hi = _head_lane_masks()
    sel_f = [jnp.where(s_, 1.0, 0.0).astype(BF16) for s_ in (lo, hi)]
    pairs = WIDTH_C // LANES
    tq = LANES
    win = 3 * tq
    rows = 2 * tq
    w_blk = jnp.maximum(i - 2, 0)
    w0 = pl.multiple_of(w_blk * tq, tq)
    q_pos = i * tq + lax.broadcasted_iota(jnp.int32, (rows, win), 0) % tq
    k_pos = w0 + lax.broadcasted_iota(jnp.int32, (rows, win), 1)
    strictly_past = k_pos < q_pos
    tj = lax.broadcasted_iota(jnp.int32, (2 * tq, 2 * tq), 0)
    ts = lax.broadcasted_iota(jnp.int32, (2 * tq, 2 * tq), 1)
    tri = jnp.where(tj >= ts, 1.0, 0.0).astype(BF16)
    tri_tot = jnp.concatenate([tri[:tq, :tq], jnp.ones((tq, tq), BF16)], axis=-1)

    def log1m_sigmoid(z):
        return -(jnp.maximum(z, 0.0) + jnp.log2(1.0 + jnp.exp2(-jnp.abs(z))))

    def put_log1m(r, log1m, width):
        lb_ref[r, :width] = log1m.astype(BF16)

    def mm(cols, w):
        return _dot(lb_ref[:, cols], w)

    for pr in range(pairs):
        cs = slice(pr * LANES, (pr + 1) * LANES)
        r = slice(pr * rows, (pr + 1) * rows)
        q = q_ref[0, :, cs]
        qs = jnp.concatenate([q * sel_f[0], q * sel_f[1]], axis=0)
        qs_ref[r, :] = qs
        z = _dot_nt(qs, k_ref[0, pl.ds(w0, win), cs])
        z_ref[r, :] = z
        put_log1m(r, jnp.where(strictly_past, log1m_sigmoid(z), 0.0), win)

    c01 = mm(slice(0, 2 * tq), tri)
    c2 = mm(slice(2 * tq, win), tri_tot)
    tot2 = c2[:, tq:]
    cum = jnp.concatenate([c01[:, :tq] + tot2, c01[:, tq:] + tot2, c2[:, :tq]], axis=-1)
    carry_ref[...] = jnp.broadcast_to(cum[:, 0:1], (pairs * rows, LANES))
    for pr in range(pairs):
        cs = slice(pr * LANES, (pr + 1) * LANES)
        r = slice(pr * rows, (pr + 1) * rows)
        a = jnp.where(strictly_past, jnp.exp2(z_ref[r, :] + cum[r, :]), 0.0)
        acc_ref[r, :] = _dot(a.astype(BF16), v_ref[0, pl.ds(w0, win), cs])

    def alive():
        return jnp.max(carry_ref[...]) > SB_DEAD

    def body(state):
        n, _ = state
        n0 = pl.multiple_of(n * tq, tq)
        for pr in range(pairs):
            cs = slice(pr * LANES, (pr + 1) * LANES)
            r = slice(pr * rows, (pr + 1) * rows)
            z = _dot_nt(qs_ref[r, :], k_ref[0, pl.ds(n0, tq), cs])
            z_ref[r, :tq] = z
            put_log1m(r, log1m_sigmoid(z), tq)
        c = mm(slice(0, tq), tri_tot)
        carry = carry_ref[...]
        for pr in range(pairs):
            cs = slice(pr * LANES, (pr + 1) * LANES)
            r = slice(pr * rows, (pr + 1) * rows)
            a = jnp.exp2(z_ref[r, :tq] + c[r, :tq] + carry[r, :])
            acc_ref[r, :] = acc_ref[r, :] + _dot(a.astype(BF16), v_ref[0, pl.ds(n0, tq), cs])
        carry_ref[...] = carry + c[:, tq:]
        return n - 1, alive()

    lax.while_loop(lambda st: jnp.logical_and(st[0] >= 0, st[1]), body, (w_blk - 1, alive()))
    for pr in range(pairs):
        o_ref[0, :, pr * LANES:(pr + 1) * LANES] = jnp.where(
            lo, acc_ref[pr * rows:pr * rows + tq, :],
            acc_ref[pr * rows + tq:(pr + 1) * rows, :]).astype(o_ref.dtype)


def _stick_breaking(arr):
    B, S, _ = arr.shape
    tq = LANES
    win = 3 * tq
    assert S >= win and S % tq == 0
    grid = (B, S // tq)
    full = (1, S, WIDTH_C)
    tile = (1, tq, WIDTH_C)
    in_specs = [
        pl.BlockSpec(tile, lambda b, i: (b, i, 0)),
        pl.BlockSpec(full, lambda b, i: (b, 0, 1)),
        pl.BlockSpec(full, lambda b, i: (b, 0, 2)),
    ]
    nrows = N_HEADS_C * tq
    stat = pltpu.VMEM((nrows, LANES), F32)
    return pl.pallas_call(
        _sb_kernel,
        grid=grid, in_specs=in_specs,
        out_specs=pl.BlockSpec(tile, lambda b, i: (b, i, 0)),
        out_shape=jax.ShapeDtypeStruct((B, S, WIDTH_C), BF16),
        scratch_shapes=[
            pltpu.VMEM((nrows, LANES), BF16),
            pltpu.VMEM((nrows, win), F32),
            pltpu.VMEM((nrows, win), BF16),
            stat, stat,
        ],
        compiler_params=pltpu.CompilerParams(
            dimension_semantics=("parallel", "parallel"), vmem_limit_bytes=VMEM_LIMIT),
        name="stick_breaking",
    )(arr, arr, arr)


def _out_mlp_kernel(x_ref, o1_ref, l1_ref, o2_ref, l2_ref, o3_ref, l3_ref, ob_ref, oc_ref,
                    gate_ref, wa_ref, wb_ref, wc_ref, wo_ref, g_ref, wu_ref, wd_ref, gf_ref,
                    y_ref, so2_ref, sl2_ref, so3_ref, sl3_ref, acc_ref,
                    *, tm, ff_chunk, final_norm):
    halves = WIDTH_A // LANES

    def gather_rows(o_ref, l_ref, so_ref, sl_ref, d):
        for r in range(d):
            for half in range(halves):
                c0 = r * WIDTH_A + half * LANES
                so_ref[half, pl.ds(r, tm // d, stride=d), :] = o_ref[0, :, c0:c0 + LANES].astype(F32)
                sl_ref[half, pl.ds(r, tm // d, stride=d), :] = l_ref[0, :, c0:c0 + LANES]
        return (jnp.concatenate([so_ref[half] for half in range(halves)], axis=-1),
                jnp.concatenate([sl_ref[half] for half in range(halves)], axis=-1))

    o1, l1 = o1_ref[0].astype(F32), l1_ref[0]
    o2, l2 = gather_rows(o2_ref, l2_ref, so2_ref, sl2_ref, DIL_GROUPS[1][1])
    o3, l3 = gather_rows(o3_ref, l3_ref, so3_ref, sl3_ref, DIL_GROUPS[2][1])
    mx = jnp.maximum(jnp.maximum(l1, l2), l3)
    e1, e2, e3 = jnp.exp(l1 - mx), jnp.exp(l2 - mx), jnp.exp(l3 - mx)
    oa = (e1 * o1 + e2 * o2 + e3 * o3) / (e1 + e2 + e3)

    ya = _dot(oa.astype(BF16), wa_ref[...])
    yb = _dot(ob_ref[0], wb_ref[...])
    yc = _dot(oc_ref[0], wc_ref[...])
    D = D_MODEL
    merged = (gate_ref[0, :, 0:D].astype(F32) * ya + gate_ref[0, :, D:2 * D].astype(F32) * yb
              + gate_ref[0, :, 2 * D:3 * D].astype(F32) * yc)
    x = x_ref[0] + _dot(merged.astype(BF16), wo_ref[...])

    ms = jnp.mean(x * x, axis=-1, keepdims=True)
    h = (x * lax.rsqrt(ms + NORM_EPS) * g_ref[...]).astype(BF16)
    for c in range(D_FF // ff_chunk):
        u = _dot(h, wu_ref[:, c * ff_chunk:(c + 1) * ff_chunk])
        u = jnp.square(jnp.maximum(u, 0.0)).astype(BF16)
        part = _dot(u, wd_ref[c * ff_chunk:(c + 1) * ff_chunk, :])
        if c == 0:
            acc_ref[...] = x + part
        else:
            acc_ref[...] += part
    y = acc_ref[...]
    if final_norm:
        ms = jnp.mean(y * y, axis=-1, keepdims=True)
        y = y * lax.rsqrt(ms + NORM_EPS) * gf_ref[...]
    y_ref[0] = y


def _out_mlp(x, o1, l1, o2, l2, o3, l3, ob, oc, gates, wa, wb, wc, wo, g_mlp, wu, wd, gf,
             *, tm, final_norm):
    B, S, D = x.shape
    d2, d3 = DIL_GROUPS[1][1], DIL_GROUPS[2][1]
    row = lambda b, i: (b, i, 0)
    const = lambda b, i: (0, 0)
    weight = lambda w: pl.BlockSpec(w.shape, const, pipeline_mode=pl.Buffered(1))
    in_specs = [
        pl.BlockSpec((1, tm, D), row),
        pl.BlockSpec((1, tm, WIDTH_A), row), pl.BlockSpec((1, tm, WIDTH_A), row),
        pl.BlockSpec((1, tm // d2, d2 * WIDTH_A), row), pl.BlockSpec((1, tm // d2, d2 * WIDTH_A), row),
        pl.BlockSpec((1, tm // d3, d3 * WIDTH_A), row), pl.BlockSpec((1, tm // d3, d3 * WIDTH_A), row),
        pl.BlockSpec((1, tm, WIDTH_B), row), pl.BlockSpec((1, tm, WIDTH_C), row),
        pl.BlockSpec((1, tm, N_BRANCHES * D), row),
        weight(wa), weight(wb), weight(wc), weight(wo),
        pl.BlockSpec((1, D), const), weight(wu), weight(wd), pl.BlockSpec((1, D), const),
    ]
    return pl.pallas_call(
        functools.partial(_out_mlp_kernel, tm=tm, ff_chunk=512, final_norm=final_norm),
        grid=(B, S // tm), in_specs=in_specs,
        out_specs=pl.BlockSpec((1, tm, D), row),
        out_shape=jax.ShapeDtypeStruct((B, S, D), F32),
        scratch_shapes=[pltpu.VMEM((WIDTH_A // LANES, tm, LANES), F32)] * 4
        + [pltpu.VMEM((tm, D), F32)],
        compiler_params=pltpu.CompilerParams(
            dimension_semantics=("parallel", "parallel"), vmem_limit_bytes=VMEM_LIMIT),
        name="out_mlp",
    )(x, o1, l1, o2, l2, o3, l3, ob, oc, gates, wa, wb, wc, wo, g_mlp, wu, wd, gf)


def _rope_tables(S):
    half = ROPE_DIM // 2
    inv_freq = jnp.exp(-math.log(ROPE_THETA) * jnp.arange(0, ROPE_DIM, 2, dtype=F32) / ROPE_DIM)
    ang = jnp.arange(S, dtype=F32)[:, None] * inv_freq[None, :]
    cos, sin = jnp.cos(ang), jnp.sin(ang)
    zeros = jnp.zeros((S, HEAD_DIM - ROPE_DIM), F32)
    zhalf = jnp.zeros((S, half), F32)
    cos_t = jnp.concatenate([cos, cos, jnp.ones_like(zeros)], axis=-1)
    up_t = jnp.concatenate([zhalf, sin, zeros], axis=-1)
    dn_t = jnp.concatenate([-sin, zhalf, zeros], axis=-1)
    tab = jnp.stack([cos_t, up_t, dn_t], axis=0)
    return jnp.concatenate([tab, tab], axis=-1)


def _layer(x, g_mix, w_in, wa, wb, wc, wo, g_mlp, wu, wd, g_final, rope_tab, *, final_norm):
    B, S, D = x.shape
    a1, a2, a3, qkv_b, qkv_c, gates = _in_proj(x, g_mix, w_in, rope_tab, tm=512)
    o1, l1 = _dilated(a1, DIL_GROUPS[0][1])
    o2, l2 = _dilated(a2, DIL_GROUPS[1][1])
    o3, l3 = _dilated(a3, DIL_GROUPS[2][1])
    ob = _moba(qkv_b)
    oc = _stick_breaking(qkv_c)
    return _out_mlp(x, o1, l1, o2, l2, o3, l3, ob, oc, gates, wa, wb, wc, wo, g_mlp, wu, wd,
                    g_final, tm=512, final_norm=final_norm)


def kernel(x, norm_mix, w_in, w_out_a, w_out_b, w_out_c, w_o, norm_mlp, w_up, w_down, norm_final):
    B, S, D = x.shape
    depth = w_in.shape[0]
    assert D == D_MODEL and S % (DIL_GROUPS[-1][0]) == 0 and S % MOBA_BLOCK == 0
    rope_tab = _rope_tables(S)
    gf = norm_final.reshape(1, D)
    for layer in range(depth):
        x = _layer(
            x, norm_mix[layer].reshape(1, D), w_in[layer].astype(BF16),
            w_out_a[layer].astype(BF16), w_out_b[layer].astype(BF16), w_out_c[layer].astype(BF16),
            w_o[layer].astype(BF16), norm_mlp[layer].reshape(1, D),
            w_up[layer].astype(BF16), w_down[layer].astype(BF16), gf, rope_tab,
            final_norm=(layer == depth - 1))
    return x
```

```python
import functools
import math

import jax
import jax.numpy as jnp
from jax import lax
from jax.experimental import pallas as pl
from jax.experimental.pallas import tpu as pltpu

D_MODEL = 1024
HEAD_DIM = 64
DIL_GROUPS = ((128, 1), (512, 4), (2048, 16))
HEADS_PER_DIL_GROUP = 4
N_HEADS_B = 6
N_HEADS_C = 6
N_HEADS = HEADS_PER_DIL_GROUP * len(DIL_GROUPS) + N_HEADS_B + N_HEADS_C
MIX_WIDTH = N_HEADS * HEAD_DIM
WIDTH_A = HEADS_PER_DIL_GROUP * HEAD_DIM
WIDTH_B = N_HEADS_B * HEAD_DIM
WIDTH_C = N_HEADS_C * HEAD_DIM
ROPE_DIM = HEAD_DIM // 4
ROPE_THETA = 500000.0
MOBA_BLOCK = 256
MOBA_TOPK = 3
D_FF = 4 * D_MODEL
N_BRANCHES = 3
NORM_EPS = 1e-6
DIL_SPAN = 128

LANES = 128
VMEM_LIMIT = 56 * 1024 * 1024

BF16 = jnp.bfloat16
F32 = jnp.float32
NEG_INF = float("-inf")
LOG2E = math.log2(math.e)
LN2 = math.log(2.0)


def _dot(a, b):
    return jnp.dot(a, b, preferred_element_type=F32)


def _dot_nt(a, b):
    return lax.dot_general(a, b, (((1,), (1,)), ((), ())), preferred_element_type=F32)


def _head_lane_masks():
    lane = lax.broadcasted_iota(jnp.int32, (1, LANES), 1)
    lo = lane < HEAD_DIM
    return lo, jnp.logical_not(lo)


def _in_proj_kernel(x_ref, g_ref, w_ref, rope_ref, a1_ref, a2_ref, a3_ref, b_ref, c_ref,
                    gate_ref, scr_ref, *, tm):
    x = x_ref[0]
    ms = jnp.mean(x * x, axis=-1, keepdims=True)
    h = (x * lax.rsqrt(ms + NORM_EPS) * g_ref[...]).astype(BF16)
    cos = rope_ref[0]
    sin_up = rope_ref[1]
    sin_dn = rope_ref[2]

    def rope(t):
        return (t * cos + pltpu.roll(t, ROPE_DIM // 2, 1) * sin_up
                + pltpu.roll(t, LANES - ROPE_DIM // 2, 1) * sin_dn)

    scale = HEAD_DIM ** -0.5
    for sec in range(3):
        base = sec * MIX_WIDTH
        for blk in range(MIX_WIDTH // 256):
            t2 = _dot(h, w_ref[:, base + blk * 256: base + (blk + 1) * 256])
            for half in range(2):
                col = blk * 256 + half * LANES
                t = t2[:, half * LANES:(half + 1) * LANES]
                if sec < 2 and col < MIX_WIDTH - WIDTH_C:
                    t = rope(t)
                if sec == 0:
                    t = t * (scale * LOG2E)
                if col < WIDTH_A:
                    a1_ref[0, :, sec * WIDTH_A + col: sec * WIDTH_A + col + LANES] = t.astype(BF16)
                elif col < 3 * WIDTH_A:
                    scr_ref[...] = t
                    grp = col // WIDTH_A
                    d = DIL_GROUPS[grp][1]
                    out_ref = a2_ref if grp == 1 else a3_ref
                    for r in range(d):
                        c0 = r * 3 * WIDTH_A + sec * WIDTH_A + col % WIDTH_A
                        out_ref[0, :, c0:c0 + LANES] = (
                            scr_ref[pl.ds(r, tm // d, stride=d), :].astype(BF16))
                elif col < 3 * WIDTH_A + WIDTH_B:
                    c0 = sec * WIDTH_B + col - 3 * WIDTH_A
                    b_ref[0, :, c0:c0 + LANES] = t.astype(BF16)
                else:
                    c0 = sec * WIDTH_C + col - 3 * WIDTH_A - WIDTH_B
                    c_ref[0, :, c0:c0 + LANES] = t.astype(BF16)
    gbase = 3 * MIX_WIDTH
    for blk in range(N_BRANCHES * D_MODEL // 256):
        t2 = _dot(h, w_ref[:, gbase + blk * 256: gbase + (blk + 1) * 256])
        gate_ref[0, :, blk * 256:(blk + 1) * 256] = jax.nn.sigmoid(t2).astype(BF16)


def _in_proj(x, g, w_in, rope_tab, *, tm):
    B, S, D = x.shape
    grid = (B, S // tm)
    d2, d3 = DIL_GROUPS[1][1], DIL_GROUPS[2][1]
    out_shape = (
        jax.ShapeDtypeStruct((B, S, 3 * WIDTH_A), BF16),
        jax.ShapeDtypeStruct((B, S // d2, d2 * 3 * WIDTH_A), BF16),
        jax.ShapeDtypeStruct((B, S // d3, d3 * 3 * WIDTH_A), BF16),
        jax.ShapeDtypeStruct((B, S, 3 * WIDTH_B), BF16),
        jax.ShapeDtypeStruct((B, S, 3 * WIDTH_C), BF16),
        jax.ShapeDtypeStruct((B, S, N_BRANCHES * D_MODEL), BF16),
    )
    row = lambda b, i: (b, i, 0)
    out_specs = (
        pl.BlockSpec((1, tm, 3 * WIDTH_A), row),
        pl.BlockSpec((1, tm // d2, d2 * 3 * WIDTH_A), row),
        pl.BlockSpec((1, tm // d3, d3 * 3 * WIDTH_A), row),
        pl.BlockSpec((1, tm, 3 * WIDTH_B), row),
        pl.BlockSpec((1, tm, 3 * WIDTH_C), row),
        pl.BlockSpec((1, tm, N_BRANCHES * D_MODEL), row),
    )
    in_specs = [
        pl.BlockSpec((1, tm, D), row),
        pl.BlockSpec((1, D), lambda b, i: (0, 0)),
        pl.BlockSpec(w_in.shape, lambda b, i: (0, 0), pipeline_mode=pl.Buffered(1)),
        pl.BlockSpec((3, tm, LANES), lambda b, i: (0, i, 0)),
    ]
    return pl.pallas_call(
        functools.partial(_in_proj_kernel, tm=tm),
        grid=grid, in_specs=in_specs, out_specs=out_specs, out_shape=out_shape,
        scratch_shapes=[pltpu.VMEM((tm, LANES), F32)],
        compiler_params=pltpu.CompilerParams(
            dimension_semantics=("parallel", "parallel"), vmem_limit_bytes=VMEM_LIMIT),
        name="in_proj",
    )(x, g, w_in, rope_tab)


def _dil_kernel(qkv_ref, o_ref, lse_ref, *bufs, nb, nres):
    lo, hi = _head_lane_masks()
    sel_f = [jnp.where(s_, 1.0, 0.0).astype(BF16) for s_ in (lo, hi)]
    blk = DIL_SPAN
    pairs = WIDTH_A // LANES
    nchain = len(bufs) // 2
    s_bufs, p_bufs = bufs[:nchain], bufs[nchain:]
    qi = lax.broadcasted_iota(jnp.int32, (2 * blk, 2 * blk), 0) % blk
    kj = lax.broadcasted_iota(jnp.int32, (2 * blk, 2 * blk), 1)
    band = jnp.logical_and(kj >= qi, kj <= qi + blk)
    first = kj <= qi
    ones = jnp.ones((2 * blk, LANES), BF16)

    def attend(blocks):
        chains = [(r0, p0, mask, res, pr) for res in range(nres) for (r0, p0, mask) in blocks
                  for pr in range(pairs)]
        for g in range(0, len(chains), nchain):
            attend_chains(chains[g:g + nchain])

    def attend_chains(chains):
        for c, (r0, p0, mask, res, pr) in enumerate(chains):
            c0 = res * 3 * WIDTH_A + pr * LANES
            q = qkv_ref[0, pl.ds(r0, blk), c0:c0 + LANES]
            kc = qkv_ref[0, pl.ds(p0, 2 * blk), c0 + WIDTH_A:c0 + WIDTH_A + LANES]
            qs = jnp.concatenate([q * sel_f[0], q * sel_f[1]], axis=0)
            s_bufs[c][...] = jnp.where(mask, _dot_nt(qs, kc), NEG_INF)
        maxes = []
        for c in range(len(chains)):
            s = s_bufs[c][...]
            m = jnp.max(s, axis=-1, keepdims=True)
            maxes.append(m)
            p_bufs[c][...] = jnp.exp2(s - m).astype(BF16)
        for c, (r0, p0, mask, res, pr) in enumerate(chains):
            c0 = res * 3 * WIDTH_A + pr * LANES + 2 * WIDTH_A
            vc = qkv_ref[0, pl.ds(p0, 2 * blk), c0:c0 + LANES]
            ov = _dot(p_bufs[c][...], jnp.concatenate([vc, ones], axis=-1))
            o = jnp.where(lo, ov[:blk, :LANES], ov[blk:, :LANES])
            l = jnp.where(lo, ov[:blk, LANES:], ov[blk:, LANES:])
            m = jnp.where(lo, maxes[c][:blk], maxes[c][blk:])
            cs = slice(res * WIDTH_A + pr * LANES, res * WIDTH_A + (pr + 1) * LANES)
            o_ref[0, pl.ds(r0, blk), cs] = (o / l).astype(o_ref.dtype)
            lse_ref[0, pl.ds(r0, blk), cs] = m * LN2 + jnp.log(l)

    def body(t, carry):
        p0 = pl.multiple_of(2 * t * blk, blk)
        r0 = pl.multiple_of((2 * t + 1) * blk, blk)
        r1 = pl.multiple_of((2 * t + 2) * blk, blk)
        attend([(r0, p0, band), (r1, r0, band)])
        return carry

    if nb == 2:
        attend([(0, 0, first), (blk, 0, band)])
    else:
        attend([(0, 0, first)])
        lax.fori_loop(0, (nb - 2) // 2, body, 0)
        attend([((nb - 1) * blk, (nb - 2) * blk, band)])


def _dilated(arr, d):
    B, Sd, _ = arr.shape
    nb = Sd // DIL_SPAN
    assert nb >= 2 and nb % 2 == 0
    nres = min(d, max(1, 2048 // Sd))
    nchain = 4
    grid = (B, d // nres)
    in_spec = pl.BlockSpec((1, Sd, nres * 3 * WIDTH_A), lambda b, r: (b, 0, r))
    out_spec = pl.BlockSpec((1, Sd, nres * WIDTH_A), lambda b, r: (b, 0, r))
    return pl.pallas_call(
        functools.partial(_dil_kernel, nb=nb, nres=nres),
        grid=grid, in_specs=[in_spec], out_specs=(out_spec, out_spec),
        out_shape=(jax.ShapeDtypeStruct((B, Sd, d * WIDTH_A), BF16),
                   jax.ShapeDtypeStruct((B, Sd, d * WIDTH_A), F32)),
        scratch_shapes=[pltpu.VMEM((2 * DIL_SPAN, 2 * DIL_SPAN), F32)] * nchain
        + [pltpu.VMEM((2 * DIL_SPAN, 2 * DIL_SPAN), BF16)] * nchain,
        compiler_params=pltpu.CompilerParams(
            dimension_semantics=("parallel", "parallel"), vmem_limit_bytes=VMEM_LIMIT),
        name=f"dilated_d{d}",
    )(arr)


MOBA_VROWS = HEAD_DIM + 16


def _moba_kernel(q_ref, k_ref, v_ref, o_ref, kmh_ref, kml_ref, vt_ref, qa_ref, m_ref, acc_ref,
                 s0_ref, s1_ref, s2_ref, p0_ref, p1_ref, p2_ref, *, nblk):
    i = pl.program_id(1)
    lo, hi = _head_lane_masks()
    sels = (lo, hi)
    bs = MOBA_BLOCK
    S = nblk * bs
    pairs = WIDTH_B // LANES
    masked = float(jnp.finfo(BF16).min)
    assert nblk < LANES
    s_refs = (s0_ref, s1_ref, s2_ref)
    p_refs = (p0_ref, p1_ref, p2_ref)
    assert pairs == len(s_refs)

    @pl.when(i == 0)
    def _():
        rows = lax.broadcasted_iota(jnp.int32, (nblk, S), 0)
        cols = lax.broadcasted_iota(jnp.int32, (nblk, S), 1)
        member = jnp.logical_and(cols >= rows * bs, cols < (rows + 1) * bs)
        ind = jnp.where(member, 1.0, 0.0).astype(BF16)
        km = _dot(ind, k_ref[0]) * (1.0 / bs)
        kmh = km.astype(BF16)
        kmh_ref[...] = kmh
        kml_ref[...] = (km - kmh.astype(F32)).astype(BF16)
        vt_ref[:, :, :, HEAD_DIM:, :] = jnp.ones(
            (pairs, 2, nblk, MOBA_VROWS - HEAD_DIM, bs), BF16)

        def transpose_block(n, carry):
            n0 = pl.multiple_of(n * bs, bs)
            for pr in range(pairs):
                vt = v_ref[0, pl.ds(n0, bs), pr * LANES:(pr + 1) * LANES].astype(F32).T
                vt_ref[pr, 0, n, :HEAD_DIM, :] = vt[:HEAD_DIM].astype(BF16)
                vt_ref[pr, 1, n, :HEAD_DIM, :] = vt[HEAD_DIM:].astype(BF16)
            return carry

        lax.fori_loop(0, nblk, transpose_block, 0)

    own0 = pl.multiple_of(i * bs, bs)
    kj = lax.broadcasted_iota(jnp.int32, (bs, bs), 0)
    qi = lax.broadcasted_iota(jnp.int32, (bs, bs), 1)
    causal = jnp.concatenate([kj <= qi] * 2, axis=1)
    blk_id = lax.broadcasted_iota(jnp.int32, (nblk, bs), 0)
    past = blk_id < i
    sel_f = [jnp.where(s_, 1.0, 0.0).astype(BF16) for s_ in sels]
    lane_tile = lax.broadcasted_iota(jnp.int32, (2 * bs, LANES), 1)
    row_tile = lax.broadcasted_iota(jnp.int32, (2 * bs, LANES), 0)

    for pr in range(pairs):
        cs = slice(pr * LANES, (pr + 1) * LANES)
        q = q_ref[0, :, cs]
        for h in range(2):
            qh = q * sel_f[h]
            g = _dot_nt(kmh_ref[:, cs], qh) + _dot_nt(kml_ref[:, cs], qh)
            g = jnp.where(past, g, NEG_INF)
            cnt = jnp.zeros((nblk, bs), jnp.int32)
            for mblk in range(nblk):
                gm = g[mblk:mblk + 1, :]
                beats = jnp.logical_or(gm > g, jnp.logical_and(gm == g, mblk < blk_id))
                cnt = cnt + beats.astype(jnp.int32)
            chosen = jnp.logical_and(cnt < MOBA_TOPK, past)
            bias_t = jnp.where(chosen, 0.0, masked)
            bias_q = jnp.concatenate(
                [bias_t, jnp.full((LANES - nblk, bs), masked, F32)], axis=0).T
            qa_ref[pr, h * bs:(h + 1) * bs, :] = jnp.concatenate(
                [qh, bias_q.astype(BF16)], axis=-1)
        k_own = k_ref[0, pl.ds(own0, bs), cs]
        s = jnp.where(causal, _dot_nt(k_own, qa_ref[pr, :, :LANES]), NEG_INF)
        m = jnp.max(s, axis=0, keepdims=True)
        p = jnp.exp2(s - m).astype(BF16)
        m_ref[pr] = m
        for h in range(2):
            acc_ref[pr, h] = _dot(vt_ref[pr, h, i], p[:, h * bs:(h + 1) * bs])

    def body(t, carry):
        n0 = pl.multiple_of(2 * t * bs, bs)
        second = jnp.where(2 * t + 1 < i, 2 * t + 1, LANES - 1)
        onehot = jnp.where(lane_tile == jnp.where(row_tile < bs, 2 * t, second), 1.0, 0.0)
        onehot = onehot.astype(BF16)
        for pr in range(pairs):
            cs = slice(pr * LANES, (pr + 1) * LANES)
            kb = k_ref[0, pl.ds(n0, 2 * bs), cs]
            s_refs[pr][...] = _dot_nt(jnp.concatenate([kb, onehot], axis=-1), qa_ref[pr])
        alphas = []
        for pr in range(pairs):
            s = s_refs[pr][...]
            m_old = m_ref[pr]
            m_new = jnp.maximum(m_old, jnp.max(s, axis=0, keepdims=True))
            alphas.append(jnp.exp2(m_old - m_new))
            p_refs[pr][...] = jnp.exp2(s - m_new).astype(BF16)
            m_ref[pr] = m_new
        for pr in range(pairs):
            for h in range(2):
                hs = slice(h * bs, (h + 1) * bs)
                pv = (_dot(vt_ref[pr, h, 2 * t], p_refs[pr][:bs, hs])
                      + _dot(vt_ref[pr, h, 2 * t + 1], p_refs[pr][bs:, hs]))
                acc_ref[pr, h] = alphas[pr][:, hs] * acc_ref[pr, h] + pv
        return carry

    lax.fori_loop(0, (i + 1) // 2, body, 0)
    for pr in range(pairs):
        o_t = jnp.concatenate(
            [acc_ref[pr, h, :HEAD_DIM, :] / acc_ref[pr, h, HEAD_DIM:HEAD_DIM + 1, :]
             for h in range(2)], axis=0)
        o_ref[0, :, pr * LANES:(pr + 1) * LANES] = o_t.T.astype(o_ref.dtype)


def _moba(arr):
    B, S, _ = arr.shape
    nblk = S // MOBA_BLOCK
    grid = (B, nblk)
    full = (1, S, WIDTH_B)
    tile = (1, MOBA_BLOCK, WIDTH_B)
    in_specs = [
        pl.BlockSpec(tile, lambda b, i: (b, i, 0)),
        pl.BlockSpec(full, lambda b, i: (b, 0, 1)),
        pl.BlockSpec(full, lambda b, i: (b, 0, 2)),
    ]
    pairs = WIDTH_B // LANES
    return pl.pallas_call(
        functools.partial(_moba_kernel, nblk=nblk),
        grid=grid, in_specs=in_specs,
        out_specs=pl.BlockSpec(tile, lambda b, i: (b, i, 0)),
        out_shape=jax.ShapeDtypeStruct((B, S, WIDTH_B), BF16),
        scratch_shapes=[
            pltpu.VMEM((nblk, WIDTH_B), BF16), pltpu.VMEM((nblk, WIDTH_B), BF16),
            pltpu.VMEM((pairs, 2, nblk, MOBA_VROWS, MOBA_BLOCK), BF16),
            pltpu.VMEM((pairs, 2 * MOBA_BLOCK, 2 * LANES), BF16),
            pltpu.VMEM((pairs, 1, 2 * MOBA_BLOCK), F32),
            pltpu.VMEM((pairs, 2, MOBA_VROWS, MOBA_BLOCK), F32),
        ] + [pltpu.VMEM((2 * MOBA_BLOCK, 2 * MOBA_BLOCK), F32)] * pairs
        + [pltpu.VMEM((2 * MOBA_BLOCK, 2 * MOBA_BLOCK), BF16)] * pairs,
        compiler_params=pltpu.CompilerParams(
            dimension_semantics=("parallel", "arbitrary"), vmem_limit_bytes=VMEM_LIMIT),
        name="moba",
    )(arr, arr, arr)


SB_DEAD = -153.0


def _sb_kernel(q_ref, k_ref, v_ref, o_ref, qs_ref, z_ref, lb_ref, carry_ref, acc_ref):
    i = pl.program_id(1)
    lo, hi = _head_lane_masks()
    sel_f = [jnp.where(s_, 1.0, 0.0).astype(BF16) for s_ in (lo, hi)]
    pairs = WIDTH_C // LANES
    tq = LANES
    win = 3 * tq
    rows = 2 * tq
    w_blk = jnp.maximum(i - 2, 0)
    w0 = pl.multiple_of(w_blk * tq, tq)
    q_pos = i * tq + lax.broadcasted_iota(jnp.int32, (rows, win), 0) % tq
    k_pos = w0 + lax.broadcasted_iota(jnp.int32, (rows, win), 1)
    strictly_past = k_pos < q_pos
    tj = lax.broadcasted_iota(jnp.int32, (2 * tq, 2 * tq), 0)
    ts = lax.broadcasted_iota(jnp.int32, (2 * tq, 2 * tq), 1)
    tri = jnp.where(tj >= ts, 1.0, 0.0).astype(BF16)
    tri_tot = jnp.concatenate([tri[:tq, :tq], jnp.ones((tq, tq), BF16)], axis=-1)

    def log1m_sigmoid(z):
        return -(jnp.maximum(z, 0.0) + jnp.log2(1.0 + jnp.exp2(-jnp.abs(z))))

    def put_log1m(r, log1m, width):
        lb_ref[r, :width] = log1m.astype(BF16)

    def mm(cols, w):
        return _dot(lb_ref[:, cols], w)

    for pr in range(pairs):
        cs = slice(pr * LANES, (pr + 1) * LANES)
        r = slice(pr * rows, (pr + 1) * rows)
        q = q_ref[0, :, cs]
        qs = jnp.concatenate([q * sel_f[0], q * sel_f[1]], axis=0)
        qs_ref[r, :] = qs
        z = _dot_nt(qs, k_ref[0, pl.ds(w0, win), cs])
        z_ref[r, :] = z
        put_log1m(r, jnp.where(strictly_past, log1m_sigmoid(z), 0.0), win)

    c01 = mm(slice(0, 2 * tq), tri)
    c2 = mm(slice(2 * tq, win), tri_tot)
    tot2 = c2[:, tq:]
    cum = jnp.concatenate([c01[:, :tq] + tot2, c01[:, tq:] + tot2, c2[:, :tq]], axis=-1)
    carry_ref[...] = jnp.broadcast_to(cum[:, 0:1], (pairs * rows, LANES))
    for pr in range(pairs):
        cs = slice(pr * LANES, (pr + 1) * LANES)
        r = slice(pr * rows, (pr + 1) * rows)
        a = jnp.where(strictly_past, jnp.exp2(z_ref[r, :] + cum[r, :]), 0.0)
        acc_ref[r, :] = _dot(a.astype(BF16), v_ref[0, pl.ds(w0, win), cs])

    def alive():
        return jnp.max(carry_ref[...]) > SB_DEAD

    def body(state):
        n, _ = state
        n0 = pl.multiple_of(n * tq, tq)
        for pr in range(pairs):
            cs = slice(pr * LANES, (pr + 1) * LANES)
            r = slice(pr * rows, (pr + 1) * rows)
            z = _dot_nt(qs_ref[r, :], k_ref[0, pl.ds(n0, tq), cs])
            z_ref[r, :tq] = z
            put_log1m(r, log1m_sigmoid(z), tq)
        c = mm(slice(0, tq), tri_tot)
        carry = carry_ref[...]
        for pr in range(pairs):
            cs = slice(pr * LANES, (pr + 1) * LANES)
            r = slice(pr * rows, (pr + 1) * rows)
            a = jnp.exp2(z_ref[r, :tq] + c[r, :tq] + carry[r, :])
            acc_ref[r, :] = acc_ref[r, :] + _dot(a.astype(BF16), v_ref[0, pl.ds(n0, tq), cs])
        carry_ref[...] = carry + c[:, tq:]
        return n - 1, alive()

    lax.while_loop(lambda st: jnp.logical_and(st[0] >= 0, st[1]), body, (w_blk - 1, alive()))
    for pr in range(pairs):
        o_ref[0, :, pr * LANES:(pr + 1) * LANES] = jnp.where(
            lo, acc_ref[pr * rows:pr * rows + tq, :],
            acc_ref[pr * rows + tq:(pr + 1) * rows, :]).astype(o_ref.dtype)


def _stick_breaking(arr):
    B, S, _ = arr.shape
    tq = LANES
    win = 3 * tq
    assert S >= win and S % tq == 0
    grid = (B, S // tq)
    full = (1, S, WIDTH_C)
    tile = (1, tq, WIDTH_C)
    in_specs = [
        pl.BlockSpec(tile, lambda b, i: (b, i, 0)),
        pl.BlockSpec(full, lambda b, i: (b, 0, 1)),
        pl.BlockSpec(full, lambda b, i: (b, 0, 2)),
    ]
    nrows = N_HEADS_C * tq
    stat = pltpu.VMEM((nrows, LANES), F32)
    return pl.pallas_call(
        _sb_kernel,
        grid=grid, in_specs=in_specs,
        out_specs=pl.BlockSpec(tile, lambda b, i: (b, i, 0)),
        out_shape=jax.ShapeDtypeStruct((B, S, WIDTH_C), BF16),
        scratch_shapes=[
            pltpu.VMEM((nrows, LANES), BF16),
            pltpu.VMEM((nrows, win), F32),
            pltpu.VMEM((nrows, win), BF16),
            stat, stat,
        ],
        compiler_params=pltpu.CompilerParams(
            dimension_semantics=("parallel", "parallel"), vmem_limit_bytes=VMEM_LIMIT),
        name="stick_breaking",
    )(arr, arr, arr)


def _out_mlp_kernel(x_ref, o1_ref, l1_ref, o2_ref, l2_ref, o3_ref, l3_ref, ob_ref, oc_ref,
                    gate_ref, wa_ref, wb_ref, wc_ref, wo_ref, g_ref, wu_ref, wd_ref, gf_ref,
                    y_ref, so2_ref, sl2_ref, so3_ref, sl3_ref, acc_ref,
                    *, tm, ff_chunk, final_norm):
    halves = WIDTH_A // LANES

    def gather_rows(o_ref, l_ref, so_ref, sl_ref, d):
        for r in range(d):
            for half in range(halves):
                c0 = r * WIDTH_A + half * LANES
                so_ref[half, pl.ds(r, tm // d, stride=d), :] = o_ref[0, :, c0:c0 + LANES].astype(F32)
                sl_ref[half, pl.ds(r, tm // d, stride=d), :] = l_ref[0, :, c0:c0 + LANES]
        return (jnp.concatenate([so_ref[half] for half in range(halves)], axis=-1),
                jnp.concatenate([sl_ref[half] for half in range(halves)], axis=-1))

    o1, l1 = o1_ref[0].astype(F32), l1_ref[0]
    o2, l2 = gather_rows(o2_ref, l2_ref, so2_ref, sl2_ref, DIL_GROUPS[1][1])
    o3, l3 = gather_rows(o3_ref, l3_ref, so3_ref, sl3_ref, DIL_GROUPS[2][1])
    mx = jnp.maximum(jnp.maximum(l1, l2), l3)
    e1, e2, e3 = jnp.exp(l1 - mx), jnp.exp(l2 - mx), jnp.exp(l3 - mx)
    oa = (e1 * o1 + e2 * o2 + e3 * o3) / (e1 + e2 + e3)

    ya = _dot(oa.astype(BF16), wa_ref[...])
    yb = _dot(ob_ref[0], wb_ref[...])
    yc = _dot(oc_ref[0], wc_ref[...])
    D = D_MODEL
    merged = (gate_ref[0, :, 0:D].astype(F32) * ya + gate_ref[0, :, D:2 * D].astype(F32) * yb
              + gate_ref[0, :, 2 * D:3 * D].astype(F32) * yc)
    x = x_ref[0] + _dot(merged.astype(BF16), wo_ref[...])

    ms = jnp.mean(x * x, axis=-1, keepdims=True)
    h = (x * lax.rsqrt(ms + NORM_EPS) * g_ref[...]).astype(BF16)
    for c in range(D_FF // ff_chunk):
        u = _dot(h, wu_ref[:, c * ff_chunk:(c + 1) * ff_chunk])
        u = jnp.square(jnp.maximum(u, 0.0)).astype(BF16)
        part = _dot(u, wd_ref[c * ff_chunk:(c + 1) * ff_chunk, :])
        if c == 0:
            acc_ref[...] = x + part
        else:
            acc_ref[...] += part
    y = acc_ref[...]
    if final_norm:
        ms = jnp.mean(y * y, axis=-1, keepdims=True)
        y = y * lax.rsqrt(ms + NORM_EPS) * gf_ref[...]
    y_ref[0] = y


def _out_mlp(x, o1, l1, o2, l2, o3, l3, ob, oc, gates, wa, wb, wc, wo, g_mlp, wu, wd, gf,
             *, tm, final_norm):
    B, S, D = x.shape
    d2, d3 = DIL_GROUPS[1][1], DIL_GROUPS[2][1]
    row = lambda b, i: (b, i, 0)
    const = lambda b, i: (0, 0)
    weight = lambda w: pl.BlockSpec(w.shape, const, pipeline_mode=pl.Buffered(1))
    in_specs = [
        pl.BlockSpec((1, tm, D), row),
        pl.BlockSpec((1, tm, WIDTH_A), row), pl.BlockSpec((1, tm, WIDTH_A), row),
        pl.BlockSpec((1, tm // d2, d2 * WIDTH_A), row), pl.BlockSpec((1, tm // d2, d2 * WIDTH_A), row),
        pl.BlockSpec((1, tm // d3, d3 * WIDTH_A), row), pl.BlockSpec((1, tm // d3, d3 * WIDTH_A), row),
        pl.BlockSpec((1, tm, WIDTH_B), row), pl.BlockSpec((1, tm, WIDTH_C), row),
        pl.BlockSpec((1, tm, N_BRANCHES * D), row),
        weight(wa), weight(wb), weight(wc), weight(wo),
        pl.BlockSpec((1, D), const), weight(wu), weight(wd), pl.BlockSpec((1, D), const),
    ]
    return pl.pallas_call(
        functools.partial(_out_mlp_kernel, tm=tm, ff_chunk=512, final_norm=final_norm),
        grid=(B, S // tm), in_specs=in_specs,
        out_specs=pl.BlockSpec((1, tm, D), row),
        out_shape=jax.ShapeDtypeStruct((B, S, D), F32),
        scratch_shapes=[pltpu.VMEM((WIDTH_A // LANES, tm, LANES), F32)] * 4
        + [pltpu.VMEM((tm, D), F32)],
        compiler_params=pltpu.CompilerParams(
            dimension_semantics=("parallel", "parallel"), vmem_limit_bytes=VMEM_LIMIT),
        name="out_mlp",
    )(x, o1, l1, o2, l2, o3, l3, ob, oc, gates, wa, wb, wc, wo, g_mlp, wu, wd, gf)


def _rope_tables(S):
    half = ROPE_DIM // 2
    inv_freq = jnp.exp(-math.log(ROPE_THETA) * jnp.arange(0, ROPE_DIM, 2, dtype=F32) / ROPE_DIM)
    ang = jnp.arange(S, dtype=F32)[:, None] * inv_freq[None, :]
    cos, sin = jnp.cos(ang), jnp.sin(ang)
    zeros = jnp.zeros((S, HEAD_DIM - ROPE_DIM), F32)
    zhalf = jnp.zeros((S, half), F32)
    cos_t = jnp.concatenate([cos, cos, jnp.ones_like(zeros)], axis=-1)
    up_t = jnp.concatenate([zhalf, sin, zeros], axis=-1)
    dn_t = jnp.concatenate([-sin, zhalf, zeros], axis=-1)
    tab = jnp.stack([cos_t, up_t, dn_t], axis=0)
    return jnp.concatenate([tab, tab], axis=-1)


def _layer(x, g_mix, w_in, wa, wb, wc, wo, g_mlp, wu, wd, g_final, rope_tab, *, final_norm):
    B, S, D = x.shape
    a1, a2, a3, qkv_b, qkv_c, gates = _in_proj(x, g_mix, w_in, rope_tab, tm=512)
    o1, l1 = _dilated(a1, DIL_GROUPS[0][1])
    o2, l2 = _dilated(a2, DIL_GROUPS[1][1])
    o3, l3 = _dilated(a3, DIL_GROUPS[2][1])
    ob = _moba(qkv_b)
    oc = _stick_breaking(qkv_c)
    return _out_mlp(x, o1, l1, o2, l2, o3, l3, ob, oc, gates, wa, wb, wc, wo, g_mlp, wu, wd,
                    g_final, tm=512, final_norm=final_norm)


def kernel(x, norm_mix, w_in, w_out_a, w_out_b, w_out_c, w_o, norm_mlp, w_up, w_down, norm_final):
    B, S, D = x.shape
    depth = w_in.shape[0]
    assert D == D_MODEL and S % (DIL_GROUPS[-1][0]) == 0 and S % MOBA_BLOCK == 0
    rope_tab = _rope_tables(S)
    gf = norm_final.reshape(1, D)
    for layer in range(depth):
        x = _layer(
            x, norm_mix[layer].reshape(1, D), w_in[layer].astype(BF16),
            w_out_a[layer].astype(BF16), w_out_b[layer].astype(BF16), w_out_c[layer].astype(BF16),
            w_o[layer].astype(BF16), norm_mlp[layer].reshape(1, D),
            w_up[layer].astype(BF16), w_down[layer].astype(BF16), gf, rope_tab,
            final_norm=(layer == depth - 1))
    return x
```

```python
import functools
import math

import jax
import jax.numpy as jnp
from jax import lax
from jax.experimental import pallas as pl
from jax.experimental.pallas import tpu as pltpu

D_MODEL = 1024
HEAD_DIM = 64
DIL_GROUPS = ((128, 1), (512, 4), (2048, 16))
HEADS_PER_DIL_GROUP = 4
N_HEADS_B = 6
N_HEADS_C = 6
N_HEADS = HEADS_PER_DIL_GROUP * len(DIL_GROUPS) + N_HEADS_B + N_HEADS_C
MIX_WIDTH = N_HEADS * HEAD_DIM
WIDTH_A = HEADS_PER_DIL_GROUP * HEAD_DIM
WIDTH_B = N_HEADS_B * HEAD_DIM
WIDTH_C = N_HEADS_C * HEAD_DIM
ROPE_DIM = HEAD_DIM // 4
ROPE_THETA = 500000.0
MOBA_BLOCK = 256
MOBA_TOPK = 3
D_FF = 4 * D_MODEL
N_BRANCHES = 3
NORM_EPS = 1e-6
DIL_SPAN = 128

LANES = 128
PROJ_CHUNK = 256
VMEM_LIMIT = 56 * 1024 * 1024

BF16 = jnp.bfloat16
F32 = jnp.float32
NEG_INF = float("-inf")
LOG2E = math.log2(math.e)
LN2 = math.log(2.0)


def _dot(a, b):
    return jnp.dot(a, b, preferred_element_type=F32)


def _dot_nt(a, b):
    return lax.dot_general(a, b, (((1,), (1,)), ((), ())), preferred_element_type=F32)


def _head_lane_masks():
    lane = lax.broadcasted_iota(jnp.int32, (1, LANES), 1)
    lo = lane < HEAD_DIM
    return lo, jnp.logical_not(lo)


def _in_proj_kernel(x_ref, g_ref, w_ref, rope_ref, a1_ref, a2_ref, a3_ref, b_ref, c_ref,
                    gate_ref, scr_ref, *, tm):
    x = x_ref[0]
    ms = jnp.mean(x * x, axis=-1, keepdims=True)
    h = (x * lax.rsqrt(ms + NORM_EPS) * g_ref[...]).astype(BF16)
    cos = rope_ref[0]
    sin_up = rope_ref[1]
    sin_dn = rope_ref[2]

    def rope(t):
        return (t * cos + pltpu.roll(t, ROPE_DIM // 2, 1) * sin_up
                + pltpu.roll(t, LANES - ROPE_DIM // 2, 1) * sin_dn)

    scale = HEAD_DIM ** -0.5
    def mix_chunk(sec, blk):
        base = sec * MIX_WIDTH
        t2 = _dot(h, w_ref[:, base + blk * PROJ_CHUNK: base + (blk + 1) * PROJ_CHUNK])
        for part in range(PROJ_CHUNK // LANES):
            col = blk * PROJ_CHUNK + part * LANES
            t = t2[:, part * LANES:(part + 1) * LANES]
            if sec < 2 and col < MIX_WIDTH - WIDTH_C:
                t = rope(t)
            if sec == 0:
                t = t * (scale * LOG2E)
            if col < WIDTH_A:
                a1_ref[0, :, sec * WIDTH_A + col: sec * WIDTH_A + col + LANES] = t.astype(BF16)
            elif col < 3 * WIDTH_A:
                scr_ref[...] = t
                grp = col // WIDTH_A
                d = DIL_GROUPS[grp][1]
                out_ref = a2_ref if grp == 1 else a3_ref
                for r in range(d):
                    c0 = r * 3 * WIDTH_A + sec * WIDTH_A + col % WIDTH_A
                    out_ref[0, :, c0:c0 + LANES] = (
                        scr_ref[pl.ds(r, tm // d, stride=d), :].astype(BF16))
            elif col < 3 * WIDTH_A + WIDTH_B:
                c0 = sec * WIDTH_B + col - 3 * WIDTH_A
                b_ref[0, :, c0:c0 + LANES] = t.astype(BF16)
            else:
                c0 = sec * WIDTH_C + col - 3 * WIDTH_A - WIDTH_B
                c_ref[0, :, c0:c0 + LANES] = t.astype(BF16)

    def gate_chunk(blk):
        gbase = 3 * MIX_WIDTH
        cols = slice(blk * PROJ_CHUNK, (blk + 1) * PROJ_CHUNK)
        t2 = _dot(h, w_ref[:, gbase + blk * PROJ_CHUNK: gbase + (blk + 1) * PROJ_CHUNK])
        gate_ref[0, :, cols] = jax.nn.sigmoid(t2).astype(BF16)

    mix_jobs = [(sec, blk) for sec in range(3) for blk in range(MIX_WIDTH // PROJ_CHUNK)]
    gate_jobs = list(range(N_BRANCHES * D_MODEL // PROJ_CHUNK))
    per_gate = -(-len(mix_jobs) // len(gate_jobs))
    while mix_jobs or gate_jobs:
        for job in mix_jobs[:per_gate]:
            mix_chunk(*job)
        mix_jobs = mix_jobs[per_gate:]
        if gate_jobs:
            gate_chunk(gate_jobs.pop(0))


def _in_proj(x, g, w_in, rope_tab, *, tm):
    B, S, D = x.shape
    grid = (B, S // tm)
    d2, d3 = DIL_GROUPS[1][1], DIL_GROUPS[2][1]
    out_shape = (
        jax.ShapeDtypeStruct((B, S, 3 * WIDTH_A), BF16),
        jax.ShapeDtypeStruct((B, S // d2, d2 * 3 * WIDTH_A), BF16),
        jax.ShapeDtypeStruct((B, S // d3, d3 * 3 * WIDTH_A), BF16),
        jax.ShapeDtypeStruct((B, S, 3 * WIDTH_B), BF16),
        jax.ShapeDtypeStruct((B, S, 3 * WIDTH_C), BF16),
        jax.ShapeDtypeStruct((B, S, N_BRANCHES * D_MODEL), BF16),
    )
    row = lambda b, i: (b, i, 0)
    out_specs = (
        pl.BlockSpec((1, tm, 3 * WIDTH_A), row),
        pl.BlockSpec((1, tm // d2, d2 * 3 * WIDTH_A), row),
        pl.BlockSpec((1, tm // d3, d3 * 3 * WIDTH_A), row),
        pl.BlockSpec((1, tm, 3 * WIDTH_B), row),
        pl.BlockSpec((1, tm, 3 * WIDTH_C), row),
        pl.BlockSpec((1, tm, N_BRANCHES * D_MODEL), row),
    )
    in_specs = [
        pl.BlockSpec((1, tm, D), row),
        pl.BlockSpec((1, D), lambda b, i: (0, 0)),
        pl.BlockSpec(w_in.shape, lambda b, i: (0, 0), pipeline_mode=pl.Buffered(1)),
        pl.BlockSpec((3, tm, LANES), lambda b, i: (0, i, 0)),
    ]
    return pl.pallas_call(
        functools.partial(_in_proj_kernel, tm=tm),
        grid=grid, in_specs=in_specs, out_specs=out_specs, out_shape=out_shape,
        scratch_shapes=[pltpu.VMEM((tm, LANES), F32)],
        compiler_params=pltpu.CompilerParams(
            dimension_semantics=("parallel", "parallel"), vmem_limit_bytes=VMEM_LIMIT),
        name="in_proj",
    )(x, g, w_in, rope_tab)


def _dil_kernel(qkv_ref, o_ref, lse_ref, *bufs, nb, nres):
    lo, hi = _head_lane_masks()
    sel_f = [jnp.where(s_, 1.0, 0.0).astype(BF16) for s_ in (lo, hi)]
    blk = DIL_SPAN
    pairs = WIDTH_A // LANES
    nchain = len(bufs) // 2
    s_bufs, p_bufs = bufs[:nchain], bufs[nchain:]
    qi = lax.broadcasted_iota(jnp.int32, (2 * blk, 2 * blk), 0) % blk
    kj = lax.broadcasted_iota(jnp.int32, (2 * blk, 2 * blk), 1)
    band = jnp.logical_and(kj >= qi, kj <= qi + blk)
    first = kj <= qi
    ones = jnp.ones((2 * blk, LANES), BF16)

    def attend(blocks):
        chains = [(r0, p0, mask, res, pr) for res in range(nres) for (r0, p0, mask) in blocks
                  for pr in range(pairs)]
        for g in range(0, len(chains), nchain):
            attend_chains(chains[g:g + nchain])

    def attend_chains(chains):
        for c, (r0, p0, mask, res, pr) in enumerate(chains):
            c0 = res * 3 * WIDTH_A + pr * LANES
            q = qkv_ref[0, pl.ds(r0, blk), c0:c0 + LANES]
            kc = qkv_ref[0, pl.ds(p0, 2 * blk), c0 + WIDTH_A:c0 + WIDTH_A + LANES]
            qs = jnp.concatenate([q * sel_f[0], q * sel_f[1]], axis=0)
            s_bufs[c][...] = jnp.where(mask, _dot_nt(qs, kc), NEG_INF)
        maxes = []
        for c in range(len(chains)):
            s = s_bufs[c][...]
            m = jnp.max(s, axis=-1, keepdims=True)
            maxes.append(m)
            p_bufs[c][...] = jnp.exp2(s - m).astype(BF16)
        for c, (r0, p0, mask, res, pr) in enumerate(chains):
            c0 = res * 3 * WIDTH_A + pr * LANES + 2 * WIDTH_A
            vc = qkv_ref[0, pl.ds(p0, 2 * blk), c0:c0 + LANES]
            ov = _dot(p_bufs[c][...], jnp.concatenate([vc, ones], axis=-1))
            o = jnp.where(lo, ov[:blk, :LANES], ov[blk:, :LANES])
            l = jnp.where(lo, ov[:blk, LANES:], ov[blk:, LANES:])
            m = jnp.where(lo, maxes[c][:blk], maxes[c][blk:])
            cs = slice(res * WIDTH_A + pr * LANES, res * WIDTH_A + (pr + 1) * LANES)
            o_ref[0, pl.ds(r0, blk), cs] = (o / l).astype(o_ref.dtype)
            lse_ref[0, pl.ds(r0, blk), cs] = m * LN2 + jnp.log(l)

    def body(t, carry):
        p0 = pl.multiple_of(2 * t * blk, blk)
        r0 = pl.multiple_of((2 * t + 1) * blk, blk)
        r1 = pl.multiple_of((2 * t + 2) * blk, blk)
        attend([(r0, p0, band), (r1, r0, band)])
        return carry

    if nb == 2:
        attend([(0, 0, first), (blk, 0, band)])
    else:
        attend([(0, 0, first)])
        lax.fori_loop(0, (nb - 2) // 2, body, 0)
        attend([((nb - 1) * blk, (nb - 2) * blk, band)])


def _dilated(arr, d):
    B, Sd, _ = arr.shape
    nb = Sd // DIL_SPAN
    assert nb >= 2 and nb % 2 == 0
    nres = min(d, max(1, 2048 // Sd))
    nchain = 4
    grid = (B, d // nres)
    in_spec = pl.BlockSpec((1, Sd, nres * 3 * WIDTH_A), lambda b, r: (b, 0, r))
    out_spec = pl.BlockSpec((1, Sd, nres * WIDTH_A), lambda b, r: (b, 0, r))
    return pl.pallas_call(
        functools.partial(_dil_kernel, nb=nb, nres=nres),
        grid=grid, in_specs=[in_spec], out_specs=(out_spec, out_spec),
        out_shape=(jax.ShapeDtypeStruct((B, Sd, d * WIDTH_A), BF16),
                   jax.ShapeDtypeStruct((B, Sd, d * WIDTH_A), F32)),
        scratch_shapes=[pltpu.VMEM((2 * DIL_SPAN, 2 * DIL_SPAN), F32)] * nchain
        + [pltpu.VMEM((2 * DIL_SPAN, 2 * DIL_SPAN), BF16)] * nchain,
        compiler_params=pltpu.CompilerParams(
            dimension_semantics=("parallel", "parallel"), vmem_limit_bytes=VMEM_LIMIT),
        name=f"dilated_d{d}",
    )(arr)


MOBA_VROWS = HEAD_DIM + 16


def _moba_kernel(q_ref, k_ref, v_ref, o_ref, kmh_ref, kml_ref, vt_ref, qa_ref, m_ref, acc_ref,
                 *bufs, nblk):
    i = pl.program_id(1)
    lo, hi = _head_lane_masks()
    sels = (lo, hi)
    bs = MOBA_BLOCK
    S = nblk * bs
    pairs = WIDTH_B // LANES
    masked = float(jnp.finfo(BF16).min)
    assert nblk < LANES
    s_refs, p_refs = bufs[:pairs], bufs[pairs:]
    assert len(p_refs) == pairs

    @pl.when(i == 0)
    def _():
        rows = lax.broadcasted_iota(jnp.int32, (nblk, S), 0)
        cols = lax.broadcasted_iota(jnp.int32, (nblk, S), 1)
        member = jnp.logical_and(cols >= rows * bs, cols < (rows + 1) * bs)
        ind = jnp.where(member, 1.0, 0.0).astype(BF16)
        km = _dot(ind, k_ref[0]) * (1.0 / bs)
        kmh = km.astype(BF16)
        kmh_ref[...] = kmh
        kml_ref[...] = (km - kmh.astype(F32)).astype(BF16)
        vt_ref[:, :, :, HEAD_DIM:, :] = jnp.ones(
            (pairs, 2, nblk, MOBA_VROWS - HEAD_DIM, bs), BF16)

        def transpose_block(n, carry):
            n0 = pl.multiple_of(n * bs, bs)
            for pr in range(pairs):
                vt = v_ref[0, pl.ds(n0, bs), pr * LANES:(pr + 1) * LANES].astype(F32).T
                vt_ref[pr, 0, n, :HEAD_DIM, :] = vt[:HEAD_DIM].astype(BF16)
                vt_ref[pr, 1, n, :HEAD_DIM, :] = vt[HEAD_DIM:].astype(BF16)
            return carry

        lax.fori_loop(0, nblk, transpose_block, 0)

    own0 = pl.multiple_of(i * bs, bs)
    kj = lax.broadcasted_iota(jnp.int32, (bs, bs), 0)
    qi = lax.broadcasted_iota(jnp.int32, (bs, bs), 1)
    causal = jnp.concatenate([kj <= qi] * 2, axis=1)
    blk_id = lax.broadcasted_iota(jnp.int32, (nblk, bs), 0)
    past = blk_id < i
    sel_f = [jnp.where(s_, 1.0, 0.0).astype(BF16) for s_ in sels]
    lane_tile = lax.broadcasted_iota(jnp.int32, (2 * bs, LANES), 1)
    row_tile = lax.broadcasted_iota(jnp.int32, (2 * bs, LANES), 0)

    for pr in range(pairs):
        cs = slice(pr * LANES, (pr + 1) * LANES)
        q = q_ref[0, :, cs]
        for h in range(2):
            qh = q * sel_f[h]
            g = _dot_nt(kmh_ref[:, cs], qh) + _dot_nt(kml_ref[:, cs], qh)
            g = jnp.where(past, g, NEG_INF)
            cnt = jnp.zeros((nblk, bs), jnp.int32)
            for mblk in range(nblk):
                gm = g[mblk:mblk + 1, :]
                beats = jnp.logical_or(gm > g, jnp.logical_and(gm == g, mblk < blk_id))
                cnt = cnt + beats.astype(jnp.int32)
            chosen = jnp.logical_and(cnt < MOBA_TOPK, past)
            bias_t = jnp.where(chosen, 0.0, masked)
            bias_q = jnp.concatenate(
                [bias_t, jnp.full((LANES - nblk, bs), masked, F32)], axis=0).T
            qa_ref[pr, h * bs:(h + 1) * bs, :] = jnp.concatenate(
                [qh, bias_q.astype(BF16)], axis=-1)
        k_own = k_ref[0, pl.ds(own0, bs), cs]
        s = jnp.where(causal, _dot_nt(k_own, qa_ref[pr, :, :LANES]), NEG_INF)
        m = jnp.max(s, axis=0, keepdims=True)
        p = jnp.exp2(s - m).astype(BF16)
        m_ref[pr] = m
        for h in range(2):
            acc_ref[pr, h] = _dot(vt_ref[pr, h, i], p[:, h * bs:(h + 1) * bs])

    def body(t, carry):
        n0 = pl.multiple_of(2 * t * bs, bs)
        second = jnp.where(2 * t + 1 < i, 2 * t + 1, LANES - 1)
        onehot = jnp.where(lane_tile == jnp.where(row_tile < bs, 2 * t, second), 1.0, 0.0)
        onehot = onehot.astype(BF16)
        for pr in range(pairs):
            cs = slice(pr * LANES, (pr + 1) * LANES)
            kb = k_ref[0, pl.ds(n0, 2 * bs), cs]
            s_refs[pr][...] = _dot_nt(jnp.concatenate([kb, onehot], axis=-1), qa_ref[pr])
        alphas = []
        for pr in range(pairs):
            s = s_refs[pr][...]
            m_old = m_ref[pr]
            m_new = jnp.maximum(m_old, jnp.max(s, axis=0, keepdims=True))
            alphas.append(jnp.exp2(m_old - m_new))
            p_refs[pr][...] = jnp.exp2(s - m_new).astype(BF16)
            m_ref[pr] = m_new
        for pr in range(pairs):
            for h in range(2):
                hs = slice(h * bs, (h + 1) * bs)
                pv = (_dot(vt_ref[pr, h, 2 * t], p_refs[pr][:bs, hs])
                      + _dot(vt_ref[pr, h, 2 * t + 1], p_refs[pr][bs:, hs]))
                acc_ref[pr, h] = alphas[pr][:, hs] * acc_ref[pr, h] + pv
        return carry

    lax.fori_loop(0, (i + 1) // 2, body, 0)
    for pr in range(pairs):
        o_t = jnp.concatenate(
            [acc_ref[pr, h, :HEAD_DIM, :] / acc_ref[pr, h, HEAD_DIM:HEAD_DIM + 1, :]
             for h in range(2)], axis=0)
        o_ref[0, :, pr * LANES:(pr + 1) * LANES] = o_t.T.astype(o_ref.dtype)


def _moba(arr):
    B, S, _ = arr.shape
    nblk = S // MOBA_BLOCK
    grid = (B, nblk)
    full = (1, S, WIDTH_B)
    tile = (1, MOBA_BLOCK, WIDTH_B)
    in_specs = [
        pl.BlockSpec(tile, lambda b, i: (b, i, 0)),
        pl.BlockSpec(full, lambda b, i: (b, 0, 1)),
        pl.BlockSpec(full, lambda b, i: (b, 0, 2)),
    ]
    pairs = WIDTH_B // LANES
    return pl.pallas_call(
        functools.partial(_moba_kernel, nblk=nblk),
        grid=grid, in_specs=in_specs,
        out_specs=pl.BlockSpec(tile, lambda b, i: (b, i, 0)),
        out_shape=jax.ShapeDtypeStruct((B, S, WIDTH_B), BF16),
        scratch_shapes=[
            pltpu.VMEM((nblk, WIDTH_B), BF16), pltpu.VMEM((nblk, WIDTH_B), BF16),
            pltpu.VMEM((pairs, 2, nblk, MOBA_VROWS, MOBA_BLOCK), BF16),
            pltpu.VMEM((pairs, 2 * MOBA_BLOCK, 2 * LANES), BF16),
            pltpu.VMEM((pairs, 1, 2 * MOBA_BLOCK), F32),
            pltpu.VMEM((pairs, 2, MOBA_VROWS, MOBA_BLOCK), F32),
        ] + [pltpu.VMEM((2 * MOBA_BLOCK, 2 * MOBA_BLOCK), F32)] * pairs
        + [pltpu.VMEM((2 * MOBA_BLOCK, 2 * MOBA_BLOCK), BF16)] * pairs,
        compiler_params=pltpu.CompilerParams(
            dimension_semantics=("parallel", "arbitrary"), vmem_limit_bytes=VMEM_LIMIT),
        name="moba",
    )(arr, arr, arr)


SB_DEAD = -153.0


SB_TILES = 4


def _sb_kernel(q_ref, k_ref, v_ref, o_ref, qs_all, z_all, lb_all, carry_all, acc_all):
    lo, hi = _head_lane_masks()
    sel_f = [jnp.where(s_, 1.0, 0.0).astype(BF16) for s_ in (lo, hi)]
    pairs = WIDTH_C // LANES
    tq = LANES
    win = 3 * tq
    rows = 2 * tq
    tj = lax.broadcasted_iota(jnp.int32, (2 * tq, 2 * tq), 0)
    ts = lax.broadcasted_iota(jnp.int32, (2 * tq, 2 * tq), 1)
    tri = jnp.where(tj >= ts, 1.0, 0.0).astype(BF16)
    tri_tot = jnp.concatenate([tri[:tq, :tq], jnp.ones((tq, tq), BF16)], axis=-1)

    def log1m_sigmoid(z):
        return -(jnp.maximum(z, 0.0) + jnp.log2(1.0 + jnp.exp2(-jnp.abs(z))))

    def first_window_block(t):
        return jnp.maximum(pl.program_id(1) * SB_TILES + t - 2, 0)

    def window(t):
        qs_ref, z_ref, lb_ref = qs_all.at[t], z_all.at[t], lb_all.at[t]
        i = pl.program_id(1) * SB_TILES + t
        w0 = pl.multiple_of(first_window_block(t) * tq, tq)
        q_pos = i * tq + lax.broadcasted_iota(jnp.int32, (rows, win), 0) % tq
        k_pos = w0 + lax.broadcasted_iota(jnp.int32, (rows, win), 1)
        strictly_past = k_pos < q_pos
        for pr in range(pairs):
            cs = slice(pr * LANES, (pr + 1) * LANES)
            r = slice(pr * rows, (pr + 1) * rows)
            q = q_ref[0, t * tq:(t + 1) * tq, cs]
            qs = jnp.concatenate([q * sel_f[0], q * sel_f[1]], axis=0)
            qs_ref[r, :] = qs
            z = _dot_nt(qs, k_ref[0, pl.ds(w0, win), cs])
            z_ref[r, :] = z
            lb_ref[r, :] = jnp.where(strictly_past, log1m_sigmoid(z), 0.0).astype(BF16)
        c01 = _dot(lb_ref[:, :2 * tq], tri)
        c2 = _dot(lb_ref[:, 2 * tq:], tri_tot)
        tot2 = c2[:, tq:]
        cum = jnp.concatenate([c01[:, :tq] + tot2, c01[:, tq:] + tot2, c2[:, :tq]], axis=-1)
        carry_all[t] = jnp.broadcast_to(cum[:, 0:1], (pairs * rows, LANES))
        for pr in range(pairs):
            cs = slice(pr * LANES, (pr + 1) * LANES)
            r = slice(pr * rows, (pr + 1) * rows)
            a = jnp.where(strictly_past, jnp.exp2(z_ref[r, :] + cum[r, :]), 0.0)
            acc_all[t, r, :] = _dot(a.astype(BF16), v_ref[0, pl.ds(w0, win), cs])

    def earlier_blocks(t):
        qs_ref, z_ref, lb_ref = qs_all.at[t], z_all.at[t], lb_all.at[t]
        carry_ref, acc_ref = carry_all.at[t], acc_all.at[t]

        def alive():
            return jnp.max(carry_ref[...]) > SB_DEAD

        def body(state):
            n, _ = state
            n0 = pl.multiple_of(n * tq, tq)
            for pr in range(pairs):
                cs = slice(pr * LANES, (pr + 1) * LANES)
                r = slice(pr * rows, (pr + 1) * rows)
                z = _dot_nt(qs_ref[r, :], k_ref[0, pl.ds(n0, tq), cs])
                z_ref[r, :tq] = z
                lb_ref[r, :tq] = log1m_sigmoid(z).astype(BF16)
            c = _dot(lb_ref[:, :tq], tri_tot)
            carry = carry_ref[...]
            for pr in range(pairs):
                cs = slice(pr * LANES, (pr + 1) * LANES)
                r = slice(pr * rows, (pr + 1) * rows)
                a = jnp.exp2(z_ref[r, :tq] + c[r, :tq] + carry[r, :])
                acc_ref[r, :] = acc_ref[r, :] + _dot(a.astype(BF16),
                                                     v_ref[0, pl.ds(n0, tq), cs])
            carry_ref[...] = carry + c[:, tq:]
            return n - 1, alive()

        lax.while_loop(lambda st: jnp.logical_and(st[0] >= 0, st[1]), body,
                       (first_window_block(t) - 1, alive()))

    for t in range(SB_TILES):
        window(t)
    for t in range(SB_TILES):
        earlier_blocks(t)
    for t in range(SB_TILES):
        for pr in range(pairs):
            o_ref[0, t * tq:(t + 1) * tq, pr * LANES:(pr + 1) * LANES] = jnp.where(
                lo, acc_all[t, pr * rows:pr * rows + tq, :],
                acc_all[t, pr * rows + tq:(pr + 1) * rows, :]).astype(o_ref.dtype)


def _stick_breaking(arr):
    B, S, _ = arr.shape
    tq = LANES
    win = 3 * tq
    assert S >= win and S % (SB_TILES * tq) == 0
    grid = (B, S // (SB_TILES * tq))
    full = (1, S, WIDTH_C)
    tile = (1, SB_TILES * tq, WIDTH_C)
    in_specs = [
        pl.BlockSpec(tile, lambda b, i: (b, i, 0)),
        pl.BlockSpec(full, lambda b, i: (b, 0, 1)),
        pl.BlockSpec(full, lambda b, i: (b, 0, 2)),
    ]
    nrows = N_HEADS_C * tq
    stat = pltpu.VMEM((SB_TILES, nrows, LANES), F32)
    return pl.pallas_call(
        _sb_kernel,
        grid=grid, in_specs=in_specs,
        out_specs=pl.BlockSpec(tile, lambda b, i: (b, i, 0)),
        out_shape=jax.ShapeDtypeStruct((B, S, WIDTH_C), BF16),
        scratch_shapes=[
            pltpu.VMEM((SB_TILES, nrows, LANES), BF16),
            pltpu.VMEM((SB_TILES, nrows, win), F32),
            pltpu.VMEM((SB_TILES, nrows, win), BF16),
            stat, stat,
        ],
        compiler_params=pltpu.CompilerParams(
            dimension_semantics=("parallel", "parallel"), vmem_limit_bytes=VMEM_LIMIT),
        name="stick_breaking",
    )(arr, arr, arr)


def _out_mlp_kernel(x_ref, o1_ref, l1_ref, o2_ref, l2_ref, o3_ref, l3_ref, ob_ref, oc_ref,
                    gate_ref, wa_ref, wb_ref, wc_ref, wo_ref, g_ref, wu_ref, wd_ref, gf_ref,
                    y_ref, so2_ref, sl2_ref, so3_ref, sl3_ref, acc_ref,
                    *, tm, ff_chunk, final_norm):
    halves = WIDTH_A // LANES

    def gather_rows(o_ref, l_ref, so_ref, sl_ref, d):
        for r in range(d):
            for half in range(halves):
                c0 = r * WIDTH_A + half * LANES
                so_ref[half, pl.ds(r, tm // d, stride=d), :] = o_ref[0, :, c0:c0 + LANES].astype(F32)
                sl_ref[half, pl.ds(r, tm // d, stride=d), :] = l_ref[0, :, c0:c0 + LANES]
        return (jnp.concatenate([so_ref[half] for half in range(halves)], axis=-1),
                jnp.concatenate([sl_ref[half] for half in range(halves)], axis=-1))

    o1, l1 = o1_ref[0].astype(F32), l1_ref[0]
    o2, l2 = gather_rows(o2_ref, l2_ref, so2_ref, sl2_ref, DIL_GROUPS[1][1])
    o3, l3 = gather_rows(o3_ref, l3_ref, so3_ref, sl3_ref, DIL_GROUPS[2][1])
    mx = jnp.maximum(jnp.maximum(l1, l2), l3)
    e1, e2, e3 = jnp.exp(l1 - mx), jnp.exp(l2 - mx), jnp.exp(l3 - mx)
    oa = (e1 * o1 + e2 * o2 + e3 * o3) / (e1 + e2 + e3)

    ya = _dot(oa.astype(BF16), wa_ref[...])
    yb = _dot(ob_ref[0], wb_ref[...])
    yc = _dot(oc_ref[0], wc_ref[...])
    D = D_MODEL
    merged = (gate_ref[0, :, 0:D].astype(F32) * ya + gate_ref[0, :, D:2 * D].astype(F32) * yb
              + gate_ref[0, :, 2 * D:3 * D].astype(F32) * yc)
    x = x_ref[0] + _dot(merged.astype(BF16), wo_ref[...])

    ms = jnp.mean(x * x, axis=-1, keepdims=True)
    h = (x * lax.rsqrt(ms + NORM_EPS) * g_ref[...]).astype(BF16)
    for c in range(D_FF // ff_chunk):
        u = _dot(h, wu_ref[:, c * ff_chunk:(c + 1) * ff_chunk])
        u = jnp.square(jnp.maximum(u, 0.0)).astype(BF16)
        part = _dot(u, wd_ref[c * ff_chunk:(c + 1) * ff_chunk, :])
        if c == 0:
            acc_ref[...] = x + part
        else:
            acc_ref[...] += part
    y = acc_ref[...]
    if final_norm:
        ms = jnp.mean(y * y, axis=-1, keepdims=True)
        y = y * lax.rsqrt(ms + NORM_EPS) * gf_ref[...]
    y_ref[0] = y


def _out_mlp(x, o1, l1, o2, l2, o3, l3, ob, oc, gates, wa, wb, wc, wo, g_mlp, wu, wd, gf,
             *, tm, final_norm):
    B, S, D = x.shape
    d2, d3 = DIL_GROUPS[1][1], DIL_GROUPS[2][1]
    row = lambda b, i: (b, i, 0)
    const = lambda b, i: (0, 0)
    weight = lambda w: pl.BlockSpec(w.shape, const, pipeline_mode=pl.Buffered(1))
    in_specs = [
        pl.BlockSpec((1, tm, D), row),
        pl.BlockSpec((1, tm, WIDTH_A), row), pl.BlockSpec((1, tm, WIDTH_A), row),
        pl.BlockSpec((1, tm // d2, d2 * WIDTH_A), row), pl.BlockSpec((1, tm // d2, d2 * WIDTH_A), row),
        pl.BlockSpec((1, tm // d3, d3 * WIDTH_A), row), pl.BlockSpec((1, tm // d3, d3 * WIDTH_A), row),
        pl.BlockSpec((1, tm, WIDTH_B), row), pl.BlockSpec((1, tm, WIDTH_C), row),
        pl.BlockSpec((1, tm, N_BRANCHES * D), row),
        weight(wa), weight(wb), weight(wc), weight(wo),
        pl.BlockSpec((1, D), const), weight(wu), weight(wd), pl.BlockSpec((1, D), const),
    ]
    return pl.pallas_call(
        functools.partial(_out_mlp_kernel, tm=tm, ff_chunk=512, final_norm=final_norm),
        grid=(B, S // tm), in_specs=in_specs,
        out_specs=pl.BlockSpec((1, tm, D), row),
        out_shape=jax.ShapeDtypeStruct((B, S, D), F32),
        scratch_shapes=[pltpu.VMEM((WIDTH_A // LANES, tm, LANES), F32)] * 4
        + [pltpu.VMEM((tm, D), F32)],
        compiler_params=pltpu.CompilerParams(
            dimension_semantics=("parallel", "parallel"), vmem_limit_bytes=VMEM_LIMIT),
        name="out_mlp",
    )(x, o1, l1, o2, l2, o3, l3, ob, oc, gates, wa, wb, wc, wo, g_mlp, wu, wd, gf)


def _rope_tables(S):
    half = ROPE_DIM // 2
    inv_freq = jnp.exp(-math.log(ROPE_THETA) * jnp.arange(0, ROPE_DIM, 2, dtype=F32) / ROPE_DIM)
    ang = jnp.arange(S, dtype=F32)[:, None] * inv_freq[None, :]
    cos, sin = jnp.cos(ang), jnp.sin(ang)
    zeros = jnp.zeros((S, HEAD_DIM - ROPE_DIM), F32)
    zhalf = jnp.zeros((S, half), F32)
    cos_t = jnp.concatenate([cos, cos, jnp.ones_like(zeros)], axis=-1)
    up_t = jnp.concatenate([zhalf, sin, zeros], axis=-1)
    dn_t = jnp.concatenate([-sin, zhalf, zeros], axis=-1)
    tab = jnp.stack([cos_t, up_t, dn_t], axis=0)
    return jnp.concatenate([tab, tab], axis=-1)


def _layer(x, g_mix, w_in, wa, wb, wc, wo, g_mlp, wu, wd, g_final, rope_tab, *, final_norm):
    B, S, D = x.shape
    a1, a2, a3, qkv_b, qkv_c, gates = _in_proj(x, g_mix, w_in, rope_tab, tm=512)
    o1, l1 = _dilated(a1, DIL_GROUPS[0][1])
    o2, l2 = _dilated(a2, DIL_GROUPS[1][1])
    o3, l3 = _dilated(a3, DIL_GROUPS[2][1])
    ob = _moba(qkv_b)
    oc = _stick_breaking(qkv_c)
    return _out_mlp(x, o1, l1, o2, l2, o3, l3, ob, oc, gates, wa, wb, wc, wo, g_mlp, wu, wd,
                    g_final, tm=512, final_norm=final_norm)


def kernel(x, norm_mix, w_in, w_out_a, w_out_b, w_out_c, w_o, norm_mlp, w_up, w_down, norm_final):
    B, S, D = x.shape
    depth = w_in.shape[0]
    assert D == D_MODEL and S % (DIL_GROUPS[-1][0]) == 0 and S % MOBA_BLOCK == 0
    rope_tab = _rope_tables(S)
    gf = norm_final.reshape(1, D)
    for layer in range(depth):
        x = _layer(
            x, norm_mix[layer].reshape(1, D), w_in[layer].astype(BF16),
            w_out_a[layer].astype(BF16), w_out_b[layer].astype(BF16), w_out_c[layer].astype(BF16),
            w_o[layer].astype(BF16), norm_mlp[layer].reshape(1, D),
            w_up[layer].astype(BF16), w_down[layer].astype(BF16), gf, rope_tab,
            final_norm=(layer == depth - 1))
    return x
```

```python
import functools
import math

import jax
import jax.numpy as jnp
from jax import lax
from jax.experimental import pallas as pl
from jax.experimental.pallas import tpu as pltpu

D_MODEL = 1024
HEAD_DIM = 64
DIL_GROUPS = ((128, 1), (512, 4), (2048, 16))
HEADS_PER_DIL_GROUP = 4
N_HEADS_B = 6
N_HEADS_C = 6
N_HEADS = HEADS_PER_DIL_GROUP * len(DIL_GROUPS) + N_HEADS_B + N_HEADS_C
MIX_WIDTH = N_HEADS * HEAD_DIM
WIDTH_A = HEADS_PER_DIL_GROUP * HEAD_DIM
WIDTH_B = N_HEADS_B * HEAD_DIM
WIDTH_C = N_HEADS_C * HEAD_DIM
ROPE_DIM = HEAD_DIM // 4
ROPE_THETA = 500000.0
MOBA_BLOCK = 256
MOBA_TOPK = 3
D_FF = 4 * D_MODEL
N_BRANCHES = 3
NORM_EPS = 1e-6
DIL_SPAN = 128

LANES = 128
PROJ_CHUNK = 256
VMEM_LIMIT = 56 * 1024 * 1024

BF16 = jnp.bfloat16
F32 = jnp.float32
NEG_INF = float("-inf")
LOG2E = math.log2(math.e)
LN2 = math.log(2.0)


def _dot(a, b):
    return jnp.dot(a, b, preferred_element_type=F32)


def _dot_nt(a, b):
    return lax.dot_general(a, b, (((1,), (1,)), ((), ())), preferred_element_type=F32)


def _head_lane_masks():
    lane = lax.broadcasted_iota(jnp.int32, (1, LANES), 1)
    lo = lane < HEAD_DIM
    return lo, jnp.logical_not(lo)


def _in_proj_kernel(x_ref, g_ref, w_ref, rope_ref, a1_ref, a2_ref, a3_ref, b_ref, c_ref,
                    gate_ref, scr_ref, *, tm):
    x = x_ref[0]
    ms = jnp.mean(x * x, axis=-1, keepdims=True)
    h = (x * lax.rsqrt(ms + NORM_EPS) * g_ref[...]).astype(BF16)
    cos = rope_ref[0]
    sin_up = rope_ref[1]
    sin_dn = rope_ref[2]

    def rope(t):
        return (t * cos + pltpu.roll(t, ROPE_DIM // 2, 1) * sin_up
                + pltpu.roll(t, LANES - ROPE_DIM // 2, 1) * sin_dn)

    scale = HEAD_DIM ** -0.5
    def mix_chunk(sec, blk):
        base = sec * MIX_WIDTH
        t2 = _dot(h, w_ref[:, base + blk * PROJ_CHUNK: base + (blk + 1) * PROJ_CHUNK])
        for part in range(PROJ_CHUNK // LANES):
            col = blk * PROJ_CHUNK + part * LANES
            t = t2[:, part * LANES:(part + 1) * LANES]
            if sec < 2 and col < MIX_WIDTH - WIDTH_C:
                t = rope(t)
            if sec == 0:
                t = t * (scale * LOG2E)
            if col < WIDTH_A:
                a1_ref[0, :, sec * WIDTH_A + col: sec * WIDTH_A + col + LANES] = t.astype(BF16)
            elif col < 3 * WIDTH_A:
                scr_ref[...] = t
                grp = col // WIDTH_A
                d = DIL_GROUPS[grp][1]
                out_ref = a2_ref if grp == 1 else a3_ref
                for r in range(d):
                    c0 = r * 3 * WIDTH_A + sec * WIDTH_A + col % WIDTH_A
                    out_ref[0, :, c0:c0 + LANES] = (
                        scr_ref[pl.ds(r, tm // d, stride=d), :].astype(BF16))
            elif col < 3 * WIDTH_A + WIDTH_B:
                c0 = sec * WIDTH_B + col - 3 * WIDTH_A
                b_ref[0, :, c0:c0 + LANES] = t.astype(BF16)
            else:
                c0 = sec * WIDTH_C + col - 3 * WIDTH_A - WIDTH_B
                c_ref[0, :, c0:c0 + LANES] = t.astype(BF16)

    def gate_chunk(blk):
        gbase = 3 * MIX_WIDTH
        cols = slice(blk * PROJ_CHUNK, (blk + 1) * PROJ_CHUNK)
        t2 = _dot(h, w_ref[:, gbase + blk * PROJ_CHUNK: gbase + (blk + 1) * PROJ_CHUNK])
        gate_ref[0, :, cols] = jax.nn.sigmoid(t2).astype(BF16)

    mix_jobs = [(sec, blk) for sec in range(3) for blk in range(MIX_WIDTH // PROJ_CHUNK)]
    gate_jobs = list(range(N_BRANCHES * D_MODEL // PROJ_CHUNK))
    per_gate = -(-len(mix_jobs) // len(gate_jobs))
    while mix_jobs or gate_jobs:
        for job in mix_jobs[:per_gate]:
            mix_chunk(*job)
        mix_jobs = mix_jobs[per_gate:]
        if gate_jobs:
            gate_chunk(gate_jobs.pop(0))


def _in_proj(x, g, w_in, rope_tab, *, tm):
    B, S, D = x.shape
    grid = (B, S // tm)
    d2, d3 = DIL_GROUPS[1][1], DIL_GROUPS[2][1]
    out_shape = (
        jax.ShapeDtypeStruct((B, S, 3 * WIDTH_A), BF16),
        jax.ShapeDtypeStruct((B, S // d2, d2 * 3 * WIDTH_A), BF16),
        jax.ShapeDtypeStruct((B, S // d3, d3 * 3 * WIDTH_A), BF16),
        jax.ShapeDtypeStruct((B, S, 3 * WIDTH_B), BF16),
        jax.ShapeDtypeStruct((B, S, 3 * WIDTH_C), BF16),
        jax.ShapeDtypeStruct((B, S, N_BRANCHES * D_MODEL), BF16),
    )
    row = lambda b, i: (b, i, 0)
    out_specs = (
        pl.BlockSpec((1, tm, 3 * WIDTH_A), row),
        pl.BlockSpec((1, tm // d2, d2 * 3 * WIDTH_A), row),
        pl.BlockSpec((1, tm // d3, d3 * 3 * WIDTH_A), row),
        pl.BlockSpec((1, tm, 3 * WIDTH_B), row),
        pl.BlockSpec((1, tm, 3 * WIDTH_C), row),
        pl.BlockSpec((1, tm, N_BRANCHES * D_MODEL), row),
    )
    in_specs = [
        pl.BlockSpec((1, tm, D), row),
        pl.BlockSpec((1, D), lambda b, i: (0, 0)),
        pl.BlockSpec(w_in.shape, lambda b, i: (0, 0), pipeline_mode=pl.Buffered(1)),
        pl.BlockSpec((3, tm, LANES), lambda b, i: (0, i, 0)),
    ]
    return pl.pallas_call(
        functools.partial(_in_proj_kernel, tm=tm),
        grid=grid, in_specs=in_specs, out_specs=out_specs, out_shape=out_shape,
        scratch_shapes=[pltpu.VMEM((tm, LANES), F32)],
        compiler_params=pltpu.CompilerParams(
            dimension_semantics=("parallel", "parallel"), vmem_limit_bytes=VMEM_LIMIT),
        name="in_proj",
    )(x, g, w_in, rope_tab)


def _dil_kernel(qkv_ref, o_ref, lse_ref, *bufs, nb, nres):
    lo, hi = _head_lane_masks()
    sel_f = [jnp.where(s_, 1.0, 0.0).astype(BF16) for s_ in (lo, hi)]
    blk = DIL_SPAN
    pairs = WIDTH_A // LANES
    nchain = len(bufs) // 2
    s_bufs, p_bufs = bufs[:nchain], bufs[nchain:]
    qi = lax.broadcasted_iota(jnp.int32, (2 * blk, 2 * blk), 0) % blk
    kj = lax.broadcasted_iota(jnp.int32, (2 * blk, 2 * blk), 1)
    band = jnp.logical_and(kj >= qi, kj <= qi + blk)
    first = kj <= qi
    ones = jnp.ones((2 * blk, LANES), BF16)

    def attend(blocks):
        chains = [(r0, p0, mask, res, pr) for res in range(nres) for (r0, p0, mask) in blocks
                  for pr in range(pairs)]
        for g in range(0, len(chains), nchain):
            attend_chains(chains[g:g + nchain])

    def attend_chains(chains):
        for c, (r0, p0, mask, res, pr) in enumerate(chains):
            c0 = res * 3 * WIDTH_A + pr * LANES
            q = qkv_ref[0, pl.ds(r0, blk), c0:c0 + LANES]
            kc = qkv_ref[0, pl.ds(p0, 2 * blk), c0 + WIDTH_A:c0 + WIDTH_A + LANES]
            qs = jnp.concatenate([q * sel_f[0], q * sel_f[1]], axis=0)
            s_bufs[c][...] = jnp.where(mask, _dot_nt(qs, kc), NEG_INF)
        maxes = []
        for c in range(len(chains)):
            s = s_bufs[c][...]
            m = jnp.max(s, axis=-1, keepdims=True)
            maxes.append(m)
            p_bufs[c][...] = jnp.exp2(s - m).astype(BF16)
        for c, (r0, p0, mask, res, pr) in enumerate(chains):
            c0 = res * 3 * WIDTH_A + pr * LANES + 2 * WIDTH_A
            vc = qkv_ref[0, pl.ds(p0, 2 * blk), c0:c0 + LANES]
            ov = _dot(p_bufs[c][...], jnp.concatenate([vc, ones], axis=-1))
            o = jnp.where(lo, ov[:blk, :LANES], ov[blk:, :LANES])
            l = jnp.where(lo, ov[:blk, LANES:], ov[blk:, LANES:])
            m = jnp.where(lo, maxes[c][:blk], maxes[c][blk:])
            cs = slice(res * WIDTH_A + pr * LANES, res * WIDTH_A + (pr + 1) * LANES)
            o_ref[0, pl.ds(r0, blk), cs] = (o / l).astype(o_ref.dtype)
            lse_ref[0, pl.ds(r0, blk), cs] = m * LN2 + jnp.log(l)

    def body(t, carry):
        p0 = pl.multiple_of(2 * t * blk, blk)
        r0 = pl.multiple_of((2 * t + 1) * blk, blk)
        r1 = pl.multiple_of((2 * t + 2) * blk, blk)
        attend([(r0, p0, band), (r1, r0, band)])
        return carry

    if nb == 2:
        attend([(0, 0, first), (blk, 0, band)])
    else:
        attend([(0, 0, first)])
        lax.fori_loop(0, (nb - 2) // 2, body, 0)
        attend([((nb - 1) * blk, (nb - 2) * blk, band)])


def _dilated(arr, d):
    B, Sd, _ = arr.shape
    nb = Sd // DIL_SPAN
    assert nb >= 2 and nb % 2 == 0
    nres = min(d, max(1, 2048 // Sd))
    nchain = 4
    grid = (B, d // nres)
    in_spec = pl.BlockSpec((1, Sd, nres * 3 * WIDTH_A), lambda b, r: (b, 0, r))
    out_spec = pl.BlockSpec((1, Sd, nres * WIDTH_A), lambda b, r: (b, 0, r))
    return pl.pallas_call(
        functools.partial(_dil_kernel, nb=nb, nres=nres),
        grid=grid, in_specs=[in_spec], out_specs=(out_spec, out_spec),
        out_shape=(jax.ShapeDtypeStruct((B, Sd, d * WIDTH_A), BF16),
                   jax.ShapeDtypeStruct((B, Sd, d * WIDTH_A), F32)),
        scratch_shapes=[pltpu.VMEM((2 * DIL_SPAN, 2 * DIL_SPAN), F32)] * nchain
        + [pltpu.VMEM((2 * DIL_SPAN, 2 * DIL_SPAN), BF16)] * nchain,
        compiler_params=pltpu.CompilerParams(
            dimension_semantics=("parallel", "parallel"), vmem_limit_bytes=VMEM_LIMIT),
        name=f"dilated_d{d}",
    )(arr)


MOBA_VROWS = HEAD_DIM + 16


def _moba_kernel(q_ref, k_ref, v_ref, o_ref, kmh_ref, kml_ref, vt_ref, qa_ref, m_ref, acc_ref,
                 *bufs, nblk):
    i = pl.program_id(1)
    lo, hi = _head_lane_masks()
    sels = (lo, hi)
    bs = MOBA_BLOCK
    S = nblk * bs
    pairs = WIDTH_B // LANES
    masked = float(jnp.finfo(BF16).min)
    assert nblk < LANES
    s_refs, p_refs = bufs[:pairs], bufs[pairs:]
    assert len(p_refs) == pairs

    @pl.when(i == 0)
    def _():
        rows = lax.broadcasted_iota(jnp.int32, (nblk, S), 0)
        cols = lax.broadcasted_iota(jnp.int32, (nblk, S), 1)
        member = jnp.logical_and(cols >= rows * bs, cols < (rows + 1) * bs)
        ind = jnp.where(member, 1.0, 0.0).astype(BF16)
        km = _dot(ind, k_ref[0]) * (1.0 / bs)
        kmh = km.astype(BF16)
        kmh_ref[...] = kmh
        kml_ref[...] = (km - kmh.astype(F32)).astype(BF16)
        vt_ref[:, :, :, HEAD_DIM:, :] = jnp.ones(
            (pairs, 2, nblk, MOBA_VROWS - HEAD_DIM, bs), BF16)

        def transpose_block(n, carry):
            n0 = pl.multiple_of(n * bs, bs)
            for pr in range(pairs):
                vt = v_ref[0, pl.ds(n0, bs), pr * LANES:(pr + 1) * LANES].astype(F32).T
                vt_ref[pr, 0, n, :HEAD_DIM, :] = vt[:HEAD_DIM].astype(BF16)
                vt_ref[pr, 1, n, :HEAD_DIM, :] = vt[HEAD_DIM:].astype(BF16)
            return carry

        lax.fori_loop(0, nblk, transpose_block, 0)

    own0 = pl.multiple_of(i * bs, bs)
    kj = lax.broadcasted_iota(jnp.int32, (bs, bs), 0)
    qi = lax.broadcasted_iota(jnp.int32, (bs, bs), 1)
    causal = jnp.concatenate([kj <= qi] * 2, axis=1)
    blk_id = lax.broadcasted_iota(jnp.int32, (nblk, bs), 0)
    past = blk_id < i
    sel_f = [jnp.where(s_, 1.0, 0.0).astype(BF16) for s_ in sels]
    lane_tile = lax.broadcasted_iota(jnp.int32, (2 * bs, LANES), 1)
    row_tile = lax.broadcasted_iota(jnp.int32, (2 * bs, LANES), 0)

    for pr in range(pairs):
        cs = slice(pr * LANES, (pr + 1) * LANES)
        q = q_ref[0, :, cs]
        for h in range(2):
            hs = slice(h * bs, (h + 1) * bs)
            qh_t = (q * sel_f[h]).astype(F32).T.astype(BF16)
            g = _dot(kmh_ref[:, cs], qh_t) + _dot(kml_ref[:, cs], qh_t)
            g = jnp.where(past, g, NEG_INF)
            cnt = jnp.zeros((nblk, bs), jnp.int32)
            for mblk in range(nblk):
                gm = g[mblk:mblk + 1, :]
                beats = jnp.logical_or(gm > g, jnp.logical_and(gm == g, mblk < blk_id))
                cnt = cnt + beats.astype(jnp.int32)
            chosen = jnp.logical_and(cnt < MOBA_TOPK, past)
            bias_t = jnp.where(chosen, 0.0, masked)
            qa_ref[pr, :LANES, hs] = qh_t
            qa_ref[pr, LANES:, hs] = jnp.concatenate(
                [bias_t, jnp.full((LANES - nblk, bs), masked, F32)], axis=0).astype(BF16)
        k_own = k_ref[0, pl.ds(own0, bs), cs]
        s = jnp.where(causal, _dot(k_own, qa_ref[pr, :LANES, :]), NEG_INF)
        m = jnp.max(s, axis=0, keepdims=True)
        p = jnp.exp2(s - m).astype(BF16)
        m_ref[pr] = m
        for h in range(2):
            acc_ref[pr, h] = _dot(vt_ref[pr, h, i], p[:, h * bs:(h + 1) * bs])

    def body(t, carry):
        n0 = pl.multiple_of(2 * t * bs, bs)
        second = jnp.where(2 * t + 1 < i, 2 * t + 1, LANES - 1)
        onehot = jnp.where(lane_tile == jnp.where(row_tile < bs, 2 * t, second), 1.0, 0.0)
        onehot = onehot.astype(BF16)
        for pr in range(pairs):
            cs = slice(pr * LANES, (pr + 1) * LANES)
            kb = k_ref[0, pl.ds(n0, 2 * bs), cs]
            s_refs[pr][...] = _dot(jnp.concatenate([kb, onehot], axis=-1), qa_ref[pr])
        alphas = []
        for pr in range(pairs):
            s = s_refs[pr][...]
            m_old = m_ref[pr]
            m_new = jnp.maximum(m_old, jnp.max(s, axis=0, keepdims=True))
            alphas.append(jnp.exp2(m_old - m_new))
            p_refs[pr][...] = jnp.exp2(s - m_new).astype(BF16)
            m_ref[pr] = m_new
        for pr in range(pairs):
            for h in range(2):
                hs = slice(h * bs, (h + 1) * bs)
                pv = (_dot(vt_ref[pr, h, 2 * t], p_refs[pr][:bs, hs])
                      + _dot(vt_ref[pr, h, 2 * t + 1], p_refs[pr][bs:, hs]))
                acc_ref[pr, h] = alphas[pr][:, hs] * acc_ref[pr, h] + pv
        return carry

    lax.fori_loop(0, (i + 1) // 2, body, 0)
    for pr in range(pairs):
        o_t = jnp.concatenate(
            [acc_ref[pr, h, :HEAD_DIM, :] / acc_ref[pr, h, HEAD_DIM:HEAD_DIM + 1, :]
             for h in range(2)], axis=0)
        o_ref[0, :, pr * LANES:(pr + 1) * LANES] = o_t.T.astype(o_ref.dtype)


def _moba(arr):
    B, S, _ = arr.shape
    nblk = S // MOBA_BLOCK
    grid = (B, nblk)
    full = (1, S, WIDTH_B)
    tile = (1, MOBA_BLOCK, WIDTH_B)
    in_specs = [
        pl.BlockSpec(tile, lambda b, i: (b, i, 0)),
        pl.BlockSpec(full, lambda b, i: (b, 0, 1)),
        pl.BlockSpec(full, lambda b, i: (b, 0, 2)),
    ]
    pairs = WIDTH_B // LANES
    return pl.pallas_call(
        functools.partial(_moba_kernel, nblk=nblk),
        grid=grid, in_specs=in_specs,
        out_specs=pl.BlockSpec(tile, lambda b, i: (b, i, 0)),
        out_shape=jax.ShapeDtypeStruct((B, S, WIDTH_B), BF16),
        scratch_shapes=[
            pltpu.VMEM((nblk, WIDTH_B), BF16), pltpu.VMEM((nblk, WIDTH_B), BF16),
            pltpu.VMEM((pairs, 2, nblk, MOBA_VROWS, MOBA_BLOCK), BF16),
            pltpu.VMEM((pairs, 2 * LANES, 2 * MOBA_BLOCK), BF16),
            pltpu.VMEM((pairs, 1, 2 * MOBA_BLOCK), F32),
            pltpu.VMEM((pairs, 2, MOBA_VROWS, MOBA_BLOCK), F32),
        ] + [pltpu.VMEM((2 * MOBA_BLOCK, 2 * MOBA_BLOCK), F32)] * pairs
        + [pltpu.VMEM((2 * MOBA_BLOCK, 2 * MOBA_BLOCK), BF16)] * pairs,
        compiler_params=pltpu.CompilerParams(
            dimension_semantics=("parallel", "arbitrary"), vmem_limit_bytes=VMEM_LIMIT),
        name="moba",
    )(arr, arr, arr)


SB_DEAD = -153.0


SB_TILES = 4


def _sb_kernel(q_ref, k_ref, v_ref, o_ref, qs_all, z_all, lb_all, carry_all, acc_all):
    lo, hi = _head_lane_masks()
    sel_f = [jnp.where(s_, 1.0, 0.0).astype(BF16) for s_ in (lo, hi)]
    pairs = WIDTH_C // LANES
    tq = LANES
    win = 3 * tq
    rows = 2 * tq
    tj = lax.broadcasted_iota(jnp.int32, (2 * tq, 2 * tq), 0)
    ts = lax.broadcasted_iota(jnp.int32, (2 * tq, 2 * tq), 1)
    tri = jnp.where(tj >= ts, 1.0, 0.0).astype(BF16)
    tri_tot = jnp.concatenate([tri[:tq, :tq], jnp.ones((tq, tq), BF16)], axis=-1)

    def log1m_sigmoid(z):
        return -(jnp.maximum(z, 0.0) + jnp.log2(1.0 + jnp.exp2(-jnp.abs(z))))

    def first_window_block(t):
        return jnp.maximum(pl.program_id(1) * SB_TILES + t - 2, 0)

    def window(t):
        qs_ref, z_ref, lb_ref = qs_all.at[t], z_all.at[t], lb_all.at[t]
        i = pl.program_id(1) * SB_TILES + t
        w0 = pl.multiple_of(first_window_block(t) * tq, tq)
        q_pos = i * tq + lax.broadcasted_iota(jnp.int32, (rows, win), 0) % tq
        k_pos = w0 + lax.broadcasted_iota(jnp.int32, (rows, win), 1)
        strictly_past = k_pos < q_pos
        for pr in range(pairs):
            cs = slice(pr * LANES, (pr + 1) * LANES)
            r = slice(pr * rows, (pr + 1) * rows)
            q = q_ref[0, t * tq:(t + 1) * tq, cs]
            qs = jnp.concatenate([q * sel_f[0], q * sel_f[1]], axis=0)
            qs_ref[r, :] = qs
            z = _dot_nt(qs, k_ref[0, pl.ds(w0, win), cs])
            z_ref[r, :] = z
            lb_ref[r, :] = jnp.where(strictly_past, log1m_sigmoid(z), 0.0).astype(BF16)
        c01 = _dot(lb_ref[:, :2 * tq], tri)
        c2 = _dot(lb_ref[:, 2 * tq:], tri_tot)
        tot2 = c2[:, tq:]
        cum = jnp.concatenate([c01[:, :tq] + tot2, c01[:, tq:] + tot2, c2[:, :tq]], axis=-1)
        carry_all[t] = jnp.broadcast_to(cum[:, 0:1], (pairs * rows, LANES))
        for pr in range(pairs):
            cs = slice(pr * LANES, (pr + 1) * LANES)
            r = slice(pr * rows, (pr + 1) * rows)
            a = jnp.where(strictly_past, jnp.exp2(z_ref[r, :] + cum[r, :]), 0.0)
            acc_all[t, r, :] = _dot(a.astype(BF16), v_ref[0, pl.ds(w0, win), cs])

    def earlier_blocks(t):
        qs_ref, z_ref, lb_ref = qs_all.at[t], z_all.at[t], lb_all.at[t]
        carry_ref, acc_ref = carry_all.at[t], acc_all.at[t]

        def alive():
            return jnp.max(carry_ref[...]) > SB_DEAD

        def body(state):
            n, _ = state
            n0 = pl.multiple_of(n * tq, tq)
            for pr in range(pairs):
                cs = slice(pr * LANES, (pr + 1) * LANES)
                r = slice(pr * rows, (pr + 1) * rows)
                z = _dot_nt(qs_ref[r, :], k_ref[0, pl.ds(n0, tq), cs])
                z_ref[r, :tq] = z
                lb_ref[r, :tq] = log1m_sigmoid(z).astype(BF16)
            c = _dot(lb_ref[:, :tq], tri_tot)
            carry = carry_ref[...]
            for pr in range(pairs):
                cs = slice(pr * LANES, (pr + 1) * LANES)
                r = slice(pr * rows, (pr + 1) * rows)
                a = jnp.exp2(z_ref[r, :tq] + c[r, :tq] + carry[r, :])
                acc_ref[r, :] = acc_ref[r, :] + _dot(a.astype(BF16),
                                                     v_ref[0, pl.ds(n0, tq), cs])
            carry_ref[...] = carry + c[:, tq:]
            return n - 1, alive()

        lax.while_loop(lambda st: jnp.logical_and(st[0] >= 0, st[1]), body,
                       (first_window_block(t) - 1, alive()))

    for t in range(SB_TILES):
        window(t)
    for t in range(SB_TILES):
        earlier_blocks(t)
    for t in range(SB_TILES):
        for pr in range(pairs):
            o_ref[0, t * tq:(t + 1) * tq, pr * LANES:(pr + 1) * LANES] = jnp.where(
                lo, acc_all[t, pr * rows:pr * rows + tq, :],
                acc_all[t, pr * rows + tq:(pr + 1) * rows, :]).astype(o_ref.dtype)


def _stick_breaking(arr):
    B, S, _ = arr.shape
    tq = LANES
    win = 3 * tq
    assert S >= win and S % (SB_TILES * tq) == 0
    grid = (B, S // (SB_TILES * tq))
    full = (1, S, WIDTH_C)
    tile = (1, SB_TILES * tq, WIDTH_C)
    in_specs = [
        pl.BlockSpec(tile, lambda b, i: (b, i, 0)),
        pl.BlockSpec(full, lambda b, i: (b, 0, 1)),
        pl.BlockSpec(full, lambda b, i: (b, 0, 2)),
    ]
    nrows = N_HEADS_C * tq
    stat = pltpu.VMEM((SB_TILES, nrows, LANES), F32)
    return pl.pallas_call(
        _sb_kernel,
        grid=grid, in_specs=in_specs,
        out_specs=pl.BlockSpec(tile, lambda b, i: (b, i, 0)),
        out_shape=jax.ShapeDtypeStruct((B, S, WIDTH_C), BF16),
        scratch_shapes=[
            pltpu.VMEM((SB_TILES, nrows, LANES), BF16),
            pltpu.VMEM((SB_TILES, nrows, win), F32),
            pltpu.VMEM((SB_TILES, nrows, win), BF16),
            stat, stat,
        ],
        compiler_params=pltpu.CompilerParams(
            dimension_semantics=("parallel", "parallel"), vmem_limit_bytes=VMEM_LIMIT),
        name="stick_breaking",
    )(arr, arr, arr)


def _out_mlp_kernel(x_ref, o1_ref, l1_ref, o2_ref, l2_ref, o3_ref, l3_ref, ob_ref, oc_ref,
                    gate_ref, wa_ref, wb_ref, wc_ref, wo_ref, g_ref, wu_ref, wd_ref, gf_ref,
                    y_ref, so2_ref, sl2_ref, so3_ref, sl3_ref, acc_ref,
                    *, tm, ff_chunk, final_norm):
    halves = WIDTH_A // LANES

    def gather_rows(o_ref, l_ref, so_ref, sl_ref, d):
        for r in range(d):
            for half in range(halves):
                c0 = r * WIDTH_A + half * LANES
                so_ref[half, pl.ds(r, tm // d, stride=d), :] = o_ref[0, :, c0:c0 + LANES].astype(F32)
                sl_ref[half, pl.ds(r, tm // d, stride=d), :] = l_ref[0, :, c0:c0 + LANES]
        return (jnp.concatenate([so_ref[half] for half in range(halves)], axis=-1),
                jnp.concatenate([sl_ref[half] for half in range(halves)], axis=-1))

    o1, l1 = o1_ref[0].astype(F32), l1_ref[0]
    o2, l2 = gather_rows(o2_ref, l2_ref, so2_ref, sl2_ref, DIL_GROUPS[1][1])
    o3, l3 = gather_rows(o3_ref, l3_ref, so3_ref, sl3_ref, DIL_GROUPS[2][1])
    mx = jnp.maximum(jnp.maximum(l1, l2), l3)
    e1, e2, e3 = jnp.exp(l1 - mx), jnp.exp(l2 - mx), jnp.exp(l3 - mx)
    oa = (e1 * o1 + e2 * o2 + e3 * o3) / (e1 + e2 + e3)

    ya = _dot(oa.astype(BF16), wa_ref[...])
    yb = _dot(ob_ref[0], wb_ref[...])
    yc = _dot(oc_ref[0], wc_ref[...])
    D = D_MODEL
    merged = (gate_ref[0, :, 0:D].astype(F32) * ya + gate_ref[0, :, D:2 * D].astype(F32) * yb
              + gate_ref[0, :, 2 * D:3 * D].astype(F32) * yc)
    x = x_ref[0] + _dot(merged.astype(BF16), wo_ref[...])

    ms = jnp.mean(x * x, axis=-1, keepdims=True)
    h = (x * lax.rsqrt(ms + NORM_EPS) * g_ref[...]).astype(BF16)
    for c in range(D_FF // ff_chunk):
        u = _dot(h, wu_ref[:, c * ff_chunk:(c + 1) * ff_chunk])
        u = jnp.square(jnp.maximum(u, 0.0)).astype(BF16)
        part = _dot(u, wd_ref[c * ff_chunk:(c + 1) * ff_chunk, :])
        if c == 0:
            acc_ref[...] = x + part
        else:
            acc_ref[...] += part
    y = acc_ref[...]
    if final_norm:
        ms = jnp.mean(y * y, axis=-1, keepdims=True)
        y = y * lax.rsqrt(ms + NORM_EPS) * gf_ref[...]
    y_ref[0] = y


def _out_mlp(x, o1, l1, o2, l2, o3, l3, ob, oc, gates, wa, wb, wc, wo, g_mlp, wu, wd, gf,
             *, tm, final_norm):
    B, S, D = x.shape
    d2, d3 = DIL_GROUPS[1][1], DIL_GROUPS[2][1]
    row = lambda b, i: (b, i, 0)
    const = lambda b, i: (0, 0)
    weight = lambda w: pl.BlockSpec(w.shape, const, pipeline_mode=pl.Buffered(1))
    in_specs = [
        pl.BlockSpec((1, tm, D), row),
        pl.BlockSpec((1, tm, WIDTH_A), row), pl.BlockSpec((1, tm, WIDTH_A), row),
        pl.BlockSpec((1, tm // d2, d2 * WIDTH_A), row), pl.BlockSpec((1, tm // d2, d2 * WIDTH_A), row),
        pl.BlockSpec((1, tm // d3, d3 * WIDTH_A), row), pl.BlockSpec((1, tm // d3, d3 * WIDTH_A), row),
        pl.BlockSpec((1, tm, WIDTH_B), row), pl.BlockSpec((1, tm, WIDTH_C), row),
        pl.BlockSpec((1, tm, N_BRANCHES * D), row),
        weight(wa), weight(wb), weight(wc), weight(wo),
        pl.BlockSpec((1, D), const), weight(wu), weight(wd), pl.BlockSpec((1, D), const),
    ]
    return pl.pallas_call(
        functools.partial(_out_mlp_kernel, tm=tm, ff_chunk=512, final_norm=final_norm),
        grid=(B, S // tm), in_specs=in_specs,
        out_specs=pl.BlockSpec((1, tm, D), row),
        out_shape=jax.ShapeDtypeStruct((B, S, D), F32),
        scratch_shapes=[pltpu.VMEM((WIDTH_A // LANES, tm, LANES), F32)] * 4
        + [pltpu.VMEM((tm, D), F32)],
        compiler_params=pltpu.CompilerParams(
            dimension_semantics=("parallel", "parallel"), vmem_limit_bytes=VMEM_LIMIT),
        name="out_mlp",
    )(x, o1, l1, o2, l2, o3, l3, ob, oc, gates, wa, wb, wc, wo, g_mlp, wu, wd, gf)


def _rope_tables(S):
    half = ROPE_DIM // 2
    inv_freq = jnp.exp(-math.log(ROPE_THETA) * jnp.arange(0, ROPE_DIM, 2, dtype=F32) / ROPE_DIM)
    ang = jnp.arange(S, dtype=F32)[:, None] * inv_freq[None, :]
    cos, sin = jnp.cos(ang), jnp.sin(ang)
    zeros = jnp.zeros((S, HEAD_DIM - ROPE_DIM), F32)
    zhalf = jnp.zeros((S, half), F32)
    cos_t = jnp.concatenate([cos, cos, jnp.ones_like(zeros)], axis=-1)
    up_t = jnp.concatenate([zhalf, sin, zeros], axis=-1)
    dn_t = jnp.concatenate([-sin, zhalf, zeros], axis=-1)
    tab = jnp.stack([cos_t, up_t, dn_t], axis=0)
    return jnp.concatenate([tab, tab], axis=-1)


def _layer(x, g_mix, w_in, wa, wb, wc, wo, g_mlp, wu, wd, g_final, rope_tab, *, final_norm):
    B, S, D = x.shape
    a1, a2, a3, qkv_b, qkv_c, gates = _in_proj(x, g_mix, w_in, rope_tab, tm=512)
    o1, l1 = _dilated(a1, DIL_GROUPS[0][1])
    o2, l2 = _dilated(a2, DIL_GROUPS[1][1])
    o3, l3 = _dilated(a3, DIL_GROUPS[2][1])
    ob = _moba(qkv_b)
    oc = _stick_breaking(qkv_c)
    return _out_mlp(x, o1, l1, o2, l2, o3, l3, ob, oc, gates, wa, wb, wc, wo, g_mlp, wu, wd,
                    g_final, tm=512, final_norm=final_norm)


def kernel(x, norm_mix, w_in, w_out_a, w_out_b, w_out_c, w_o, norm_mlp, w_up, w_down, norm_final):
    B, S, D = x.shape
    depth = w_in.shape[0]
    assert D == D_MODEL and S % (DIL_GROUPS[-1][0]) == 0 and S % MOBA_BLOCK == 0
    rope_tab = _rope_tables(S)
    gf = norm_final.reshape(1, D)
    for layer in range(depth):
        x = _layer(
            x, norm_mix[layer].reshape(1, D), w_in[layer].astype(BF16),
            w_out_a[layer].astype(BF16), w_out_b[layer].astype(BF16), w_out_c[layer].astype(BF16),
            w_o[layer].astype(BF16), norm_mlp[layer].reshape(1, D),
            w_up[layer].astype(BF16), w_down[layer].astype(BF16), gf, rope_tab,
            final_norm=(layer == depth - 1))
    return x
```
